```python
import math
import jax, jax.numpy as jnp
from jax import lax
import numpy as np

D_MODEL = 2048
BATCH = 1
SEQ = 16384
DEPTH = 1
DEC_BATCH = 1
DEC_SEQ = 8192
PAST_LEN = 128

GRID_W = 64
PLE_DIM = 256
F_WIDTH = 1024
F_GROUPS = 8
F_GROUP_CH = F_WIDTH // F_GROUPS
NA_HEADS = 8
NA_HEAD_DIM = 128
NA_WIDTH = NA_HEADS * NA_HEAD_DIM
WIN_H = 8
WIN_W = 16
D_FF = 5632
CONV_W = 3
IN_COLS = F_WIDTH + 3 * NA_WIDTH + 2 * D_MODEL
LN_EPS = 1e-5
DN_ALPHA = (2.0 * DEPTH) ** 0.25
DN_BETA = (8.0 * DEPTH) ** -0.25

kernel_name = "hybrid_fnet_natten_convffn_encoder"


def layer_norm(x, g, b):
    xf = x.astype(jnp.float32)
    mu = jnp.mean(xf, axis=-1, keepdims=True)
    var = jnp.mean(jnp.square(xf - mu), axis=-1, keepdims=True)
    y = (xf - mu) * lax.rsqrt(var + LN_EPS) * g.astype(jnp.float32) + b.astype(jnp.float32)
    return y.astype(x.dtype)


def fourier_mix(f_in):
    bsz, s, _ = f_in.shape
    f = f_in.reshape(bsz, s, F_GROUPS, F_GROUP_CH).astype(jnp.float32)
    fr = jnp.fft.fft2(f, axes=(1, 3), norm="ortho").real
    return fr.reshape(bsz, s, F_WIDTH).astype(f_in.dtype)


def neighbourhood_attention(q, k, v, rpb):
    bsz, s, _ = q.shape
    rows = s // GRID_W
    kh = min(WIN_H, rows)
    shp = (bsz, rows, GRID_W, NA_HEADS, NA_HEAD_DIM)
    qg, kg, vg = q.reshape(shp), k.reshape(shp), v.reshape(shp)
    scale = NA_HEAD_DIM ** -0.5
    col_start = np.clip(np.arange(GRID_W) - WIN_W // 2, 0, GRID_W - WIN_W)
    col_idx = col_start[:, None] + np.arange(WIN_W)[None, :]
    dc = col_idx - np.arange(GRID_W)[:, None]
    col_idx = jnp.asarray(col_idx, dtype=jnp.int32)
    dc_idx = jnp.asarray(dc + WIN_W - 1, dtype=jnp.int32)
    rpb_f = rpb.astype(jnp.float32)

    def row_fn(r):
        rs = jnp.clip(r - kh // 2, 0, rows - kh)
        q_r = lax.dynamic_index_in_dim(qg, r, axis=1, keepdims=False)
        k_rows = lax.dynamic_slice_in_dim(kg, rs, kh, axis=1)
        v_rows = lax.dynamic_slice_in_dim(vg, rs, kh, axis=1)
        k_win = k_rows[:, :, col_idx]
        v_win = v_rows[:, :, col_idx]
        sc = jnp.einsum('bqhd,bkqwhd->bhqkw', q_r, k_win).astype(jnp.float32) * scale
        dr = rs + jnp.arange(kh, dtype=jnp.int32) - r
        bias = rpb_f[:, dr[:, None, None] + WIN_H - 1, dc_idx[None, :, :]]
        sc = sc + jnp.transpose(bias, (0, 2, 1, 3))[None]
        p = jax.nn.softmax(sc.reshape(bsz, NA_HEADS, GRID_W, kh * WIN_W), axis=-1)
        p = p.reshape(bsz, NA_HEADS, GRID_W, kh, WIN_W).astype(v.dtype)
        return jnp.einsum('bhqkw,bkqwhd->bqhd', p, v_win)

    out = lax.map(row_fn, jnp.arange(rows, dtype=jnp.int32))
    return jnp.transpose(out, (1, 0, 2, 3, 4)).reshape(bsz, s, NA_WIDTH)


def conv3_centred(u, w, b):
    up = jnp.pad(u, ((0, 0), (1, 1), (0, 0)))
    return up[:, :-2] * w[0] + up[:, 1:-1] * w[1] + up[:, 2:] * w[2] + b


def encoder_layer(x, p, w_in, gate_b, fourier_w, natten_rpb, natten_w, w_out, ln1_g, ln1_b,
                  ffn_up, ffn_conv, ffn_conv_b, ffn_down, ple_proj, ple_gate, ln2_g, ln2_b):
    z = x @ w_in
    f_in, q, k, v, g_a, g_b = jnp.split(
        z, [F_WIDTH, F_WIDTH + NA_WIDTH, F_WIDTH + 2 * NA_WIDTH, F_WIDTH + 3 * NA_WIDTH,
            F_WIDTH + 3 * NA_WIDTH + D_MODEL], axis=-1)
    gates = jax.nn.sigmoid(jnp.concatenate([g_a, g_b], axis=-1) + gate_b)
    branch_a = fourier_mix(f_in) @ fourier_w
    branch_b = neighbourhood_attention(q, k, v, natten_rpb) @ natten_w
    merged = gates[..., :D_MODEL] * branch_a + gates[..., D_MODEL:] * branch_b
    x = layer_norm(DN_ALPHA * x + merged @ w_out, ln1_g, ln1_b)
    u = conv3_centred(x @ ffn_up, ffn_conv, ffn_conv_b)
    h = jax.nn.gelu(u[..., :D_FF]) * u[..., D_FF:]
    ffn_out = h @ ffn_down
    ple = jax.nn.sigmoid(x @ ple_gate) * (p @ ple_proj)
    return layer_norm(DN_ALPHA * x + ffn_out + ple, ln2_g, ln2_b)


def setup_inputs(seed: int = 0) -> dict:
    key = jax.random.key(seed)
    ks = jax.random.split(key, 20)
    nrm = lambda k, shp, s: jax.random.normal(k, shp, jnp.float32) * s
    return {
        "x_prompt": nrm(ks[0], (BATCH, SEQ, D_MODEL), 1.0),
        "x_sample": nrm(ks[1], (DEC_BATCH, DEC_SEQ, D_MODEL), 1.0),
        "p_prompt": nrm(ks[2], (DEPTH, BATCH, SEQ, PLE_DIM), 1.0),
        "p_sample": nrm(ks[3], (DEPTH, DEC_BATCH, DEC_SEQ, PLE_DIM), 1.0),
        "w_in": nrm(ks[4], (DEPTH, D_MODEL, IN_COLS), D_MODEL ** -0.5),
        "gate_b": nrm(ks[5], (DEPTH, 2 * D_MODEL), 0.01),
        "fourier_w": nrm(ks[6], (DEPTH, F_WIDTH, D_MODEL), F_WIDTH ** -0.5),
        "natten_rpb": nrm(ks[7], (DEPTH, NA_HEADS, 2 * WIN_H - 1, 2 * WIN_W - 1), 0.1),
        "natten_w": nrm(ks[8], (DEPTH, NA_WIDTH, D_MODEL), NA_WIDTH ** -0.5),
        "w_out": nrm(ks[9], (DEPTH, D_MODEL, D_MODEL), DN_BETA * D_MODEL ** -0.5),
        "ln1_g": 1.0 + nrm(ks[10], (DEPTH, D_MODEL), 0.02),
        "ln1_b": nrm(ks[11], (DEPTH, D_MODEL), 0.02),
        "ffn_up": nrm(ks[12], (DEPTH, D_MODEL, 2 * D_FF), D_MODEL ** -0.5),
        "ffn_conv": nrm(ks[13], (DEPTH, CONV_W, 2 * D_FF), CONV_W ** -0.5),
        "ffn_conv_b": nrm(ks[14], (DEPTH, 2 * D_FF), 0.02),
        "ffn_down": nrm(ks[15], (DEPTH, D_FF, D_MODEL), DN_BETA * D_FF ** -0.5),
        "ple_proj": nrm(ks[16], (DEPTH, PLE_DIM, D_MODEL), PLE_DIM ** -0.5),
        "ple_gate": nrm(ks[17], (DEPTH, D_MODEL, D_MODEL), D_MODEL ** -0.5),
        "ln2_g": 1.0 + nrm(ks[18], (DEPTH, D_MODEL), 0.02),
        "ln2_b": nrm(ks[19], (DEPTH, D_MODEL), 0.02),
    }


def reference(x_prompt, x_sample, p_prompt, p_sample, w_in, gate_b, fourier_w, natten_rpb, natten_w,
              w_out, ln1_g, ln1_b, ffn_up, ffn_conv, ffn_conv_b, ffn_down, ple_proj, ple_gate,
              ln2_g, ln2_b):
    xp, xs = x_prompt, x_sample
    for i in range(DEPTH):
        xp = encoder_layer(xp, p_prompt[i], w_in[i], gate_b[i], fourier_w[i], natten_rpb[i], natten_w[i],
                           w_out[i], ln1_g[i], ln1_b[i], ffn_up[i], ffn_conv[i], ffn_conv_b[i], ffn_down[i],
                           ple_proj[i], ple_gate[i], ln2_g[i], ln2_b[i])
        xs = encoder_layer(xs, p_sample[i], w_in[i], gate_b[i], fourier_w[i], natten_rpb[i], natten_w[i],
                           w_out[i], ln1_g[i], ln1_b[i], ffn_up[i], ffn_conv[i], ffn_conv_b[i], ffn_down[i],
                           ple_proj[i], ple_gate[i], ln2_g[i], ln2_b[i])
    return (xp, xs)
```

```python
import functools
import math

import jax
import jax.numpy as jnp
import numpy as np
from jax import lax
from jax.experimental import pallas as pl
from jax.experimental.pallas import tpu as pltpu

D_MODEL = 2048
GRID_W = 64
PLE_DIM = 256
F_WIDTH = 1024
F_GROUPS = 8
F_GROUP_CH = F_WIDTH // F_GROUPS
NA_HEADS = 8
NA_HEAD_DIM = 128
NA_WIDTH = NA_HEADS * NA_HEAD_DIM
WIN_H = 8
WIN_W = 16
D_FF = 5632
IN_COLS = F_WIDTH + 3 * NA_WIDTH + 2 * D_MODEL
LN_EPS = 1e-5
DEPTH = 1
DN_ALPHA = (2.0 * DEPTH) ** 0.25

VMEM_LIMIT_BYTES = 56 * 1024 * 1024
NEG_BIAS = -1e30

DFT_N2 = 128
NA_QROWS = 8
NA_KROWS = 16
NA_KROW_OFFSET = 4


def _cparams(*semantics):
    return pltpu.CompilerParams(dimension_semantics=semantics, vmem_limit_bytes=VMEM_LIMIT_BYTES)


MXU_COLS = 256


def _in_proj_kernel(x_ref, wz_ref, wg_ref, gb_ref, z_ref, g_ref, xb_ref, *, tn):
    @pl.when(pl.program_id(1) == 0)
    def _():
        xb_ref[...] = x_ref[...].astype(jnp.bfloat16)

    xb = xb_ref[...]
    for c in range(0, tn, MXU_COLS):
        sl = slice(c, c + MXU_COLS)
        acc = jnp.dot(xb, wg_ref[:, sl], preferred_element_type=jnp.float32)
        g_ref[:, sl] = jax.nn.sigmoid(acc + gb_ref[:, sl]).astype(g_ref.dtype)
    for c in range(0, tn, MXU_COLS):
        sl = slice(c, c + MXU_COLS)
        z_ref[:, sl] = jnp.dot(xb, wz_ref[:, sl], preferred_element_type=jnp.float32).astype(z_ref.dtype)


def _in_proj(x, w_in_b, gate_b, tm=1024, tn=1024):
    s = x.shape[0]
    z_cols = F_WIDTH + 3 * NA_WIDTH
    g_cols = IN_COLS - z_cols
    assert z_cols == g_cols
    n_col = z_cols // tn
    return pl.pallas_call(
        functools.partial(_in_proj_kernel, tn=tn),
        grid=(s // tm, n_col),
        in_specs=[
            pl.BlockSpec((tm, D_MODEL), lambda i, j: (i, 0)),
            pl.BlockSpec((D_MODEL, tn), lambda i, j: (0, j)),
            pl.BlockSpec((D_MODEL, tn), lambda i, j: (0, j + n_col)),
            pl.BlockSpec((1, tn), lambda i, j: (0, j)),
        ],
        out_specs=[pl.BlockSpec((tm, tn), lambda i, j: (i, j)),
                   pl.BlockSpec((tm, tn), lambda i, j: (i, j))],
        out_shape=[jax.ShapeDtypeStruct((s, z_cols), jnp.bfloat16),
                   jax.ShapeDtypeStruct((s, g_cols), jnp.bfloat16)],
        scratch_shapes=[pltpu.VMEM((tm, D_MODEL), jnp.bfloat16)],
        compiler_params=_cparams("parallel", "arbitrary"),
        name="in_proj",
    )(x, w_in_b, w_in_b, gate_b)


def _dft_tables(s):
    n2 = DFT_N2
    n1 = s // n2
    k1 = np.arange(n1)
    w1 = np.exp(-2j * np.pi * np.outer(k1, k1) / n1) / math.sqrt(n1)
    fa = np.concatenate([w1.real, w1.imag], axis=0)
    k2 = np.arange(n2)
    w2 = np.exp(-2j * np.pi * np.outer(k2, k2) / n2) / math.sqrt(n2)
    tw = np.exp(-2j * np.pi * np.outer(k1, k2) / s)
    c = np.arange(F_GROUP_CH)
    wc = np.exp(-2j * np.pi * np.outer(c, c) / F_GROUP_CH) / math.sqrt(F_GROUP_CH)
    wch = np.concatenate([wc.real, -wc.imag], axis=0)
    f32 = lambda a: jnp.asarray(a, dtype=jnp.float32)
    return (f32(fa), f32(w2.real), f32(w2.imag), f32(tw.real), f32(tw.imag), f32(wch))


def _dft_pos_a_kernel(fa_ref, x_ref, o_ref, *, slabs, n1):
    fa = fa_ref[...].astype(jnp.bfloat16)
    for j in range(slabs):
        p = jnp.dot(fa, x_ref[j], preferred_element_type=jnp.float32)
        o_ref[0, j] = p[:n1].astype(o_ref.dtype)
        o_ref[1, j] = p[n1:].astype(o_ref.dtype)


def _dft_pos_a(ft, fa, slabs=8):
    n2, n1, c = ft.shape
    return pl.pallas_call(
        functools.partial(_dft_pos_a_kernel, slabs=slabs, n1=n1),
        grid=(n2 // slabs,),
        in_specs=[
            pl.BlockSpec((2 * n1, n1), lambda i: (0, 0)),
            pl.BlockSpec((slabs, n1, c), lambda i: (i, 0, 0)),
        ],
        out_specs=pl.BlockSpec((2, slabs, n1, c), lambda i: (0, i, 0, 0)),
        out_shape=jax.ShapeDtypeStruct((2, n2, n1, c), jnp.bfloat16),
        compiler_params=_cparams("parallel"),
        name="dft_pos_a",
    )(fa, ft)


def _dft_pos_c_kernel(w2r_ref, w2i_ref, twr_ref, twi_ref, wch_ref, a_ref, o_ref, *, slabs):
    w2r = w2r_ref[...]
    w2i = w2i_ref[...]
    wch = wch_ref[...].astype(jnp.bfloat16)
    for j in range(slabs):
        tr = twr_ref[j:j + 1, :]
        ti = twi_ref[j:j + 1, :]
        gr = (w2r * tr - w2i * ti).astype(jnp.bfloat16)
        gi = (w2r * ti + w2i * tr).astype(jnp.bfloat16)
        g4 = jnp.concatenate(
            [jnp.concatenate([gr, -gi], axis=1), jnp.concatenate([gi, gr], axis=1)], axis=0)
        a = jnp.concatenate([a_ref[0, j], a_ref[1, j]], axis=0)
        y = jnp.dot(g4, a, preferred_element_type=jnp.float32).astype(jnp.bfloat16)
        n2 = y.shape[0] // 2
        yr, yi = y[:n2], y[n2:]
        for g in range(F_GROUPS):
            sl = slice(g * F_GROUP_CH, (g + 1) * F_GROUP_CH)
            lhs = jnp.concatenate([yr[:, sl], yi[:, sl]], axis=1)
            o_ref[j, :, sl] = jnp.dot(lhs, wch, preferred_element_type=jnp.float32).astype(o_ref.dtype)


def _dft_pos_c(a_t, w2r, w2i, twr, twi, wch, slabs=8):
    _, n1, n2, c = a_t.shape
    full = lambda shape: pl.BlockSpec(shape, lambda i: (0,) * len(shape))
    return pl.pallas_call(
        functools.partial(_dft_pos_c_kernel, slabs=slabs),
        grid=(n1 // slabs,),
        in_specs=[
            full((n2, n2)), full((n2, n2)),
            pl.BlockSpec((slabs, n2), lambda i: (i, 0)),
            pl.BlockSpec((slabs, n2), lambda i: (i, 0)),
            full((2 * F_GROUP_CH, F_GROUP_CH)),
            pl.BlockSpec((2, slabs, n2, c), lambda i: (0, i, 0, 0)),
        ],
        out_specs=pl.BlockSpec((slabs, n2, c), lambda i: (i, 0, 0)),
        out_shape=jax.ShapeDtypeStruct((n1, n2, c), jnp.bfloat16),
        compiler_params=_cparams("parallel"),
        name="dft_pos_c",
    )(w2r, w2i, twr, twi, wch, a_t)


def _fourier_mix(z, tables):
    s = z.shape[0]
    n2 = DFT_N2
    n1 = s // n2
    fa, w2r, w2i, twr, twi, wch = tables
    ft = jnp.transpose(z[:, :F_WIDTH].reshape(n1, n2, F_WIDTH), (1, 0, 2))
    a = _dft_pos_a(ft, fa)
    a_t = jnp.transpose(a, (0, 2, 1, 3))
    y = _dft_pos_c(a_t, w2r, w2i, twr, twi, wch)
    return jnp.transpose(y, (1, 0, 2)).reshape(s, F_WIDTH)


def _natten_bias(rpb):
    qc = np.arange(GRID_W)
    col_start = np.clip(qc - WIN_W // 2, 0, GRID_W - WIN_W)
    kc = np.arange(GRID_W)
    col_ok = (kc[None, :] >= col_start[:, None]) & (kc[None, :] < col_start[:, None] + WIN_W)
    dc_idx = np.clip(kc[None, :] - qc[:, None] + WIN_W - 1, 0, 2 * WIN_W - 2)
    cb = rpb.astype(jnp.float32)[:, :, dc_idx]
    cb = jnp.where(col_ok[None, None], cb, NEG_BIAS)

    i = np.arange(NA_QROWS)[:, None]
    j = np.arange(NA_KROWS)[None, :]
    dr = j - NA_KROW_OFFSET - i
    dr_idx = np.clip(dr + WIN_H - 1, 0, 2 * WIN_H - 2)
    lo_mid = np.full((NA_QROWS, 1), -(WIN_H // 2))
    lo_first = np.where(i < WIN_H // 2, -i, -(WIN_H // 2))
    lo_last = np.where(i > WIN_H // 2, -i, -(WIN_H // 2))
    variants = []
    for lo in (lo_first, lo_mid, lo_last):
        row_ok = (dr >= lo) & (dr <= lo + WIN_H - 1)
        b = cb[:, dr_idx]
        b = jnp.where(row_ok[None, :, :, None, None], b, NEG_BIAS)
        b = jnp.transpose(b, (0, 1, 3, 2, 4)).reshape(NA_HEADS, NA_QROWS * GRID_W, NA_KROWS * GRID_W)
        variants.append(b)
    return jnp.stack(variants, axis=0) * (NA_HEAD_DIM ** 0.5)


NA_HEADS_PER_STEP = 4


def _natten_kernel(q_ref, kp_ref, kc_ref, kn_ref, vp_ref, vc_ref, vn_ref, b_ref, o_ref):
    half = (NA_KROWS - NA_QROWS) // 2 * GRID_W
    blk = NA_QROWS * GRID_W
    exp2_scale = (NA_HEAD_DIM ** -0.5) * math.log2(math.e)
    for h in range(NA_HEADS_PER_STEP):
        sl = slice(h * NA_HEAD_DIM, (h + 1) * NA_HEAD_DIM)
        kwin = jnp.concatenate([kp_ref[blk - half:, sl], kc_ref[:, sl], kn_ref[:half, sl]], axis=0)
        vwin = jnp.concatenate([vp_ref[blk - half:, sl], vc_ref[:, sl], vn_ref[:half, sl]], axis=0)
        t = lax.dot_general(q_ref[:, sl], kwin, (((1,), (1,)), ((), ())),
                            preferred_element_type=jnp.float32) + b_ref[0, h]
        m = jnp.max(t, axis=-1, keepdims=True)
        p = jnp.exp2((t - m) * exp2_scale)
        l = jnp.sum(p, axis=-1, keepdims=True)
        o = jnp.dot(p.astype(jnp.bfloat16), vwin, preferred_element_type=jnp.float32)
        o_ref[:, sl] = (o / l).astype(o_ref.dtype)


def _natten(z, bias):
    s = z.shape[0]
    blk = NA_QROWS * GRID_W
    nb = s // blk
    width = NA_HEADS_PER_STEP * NA_HEAD_DIM
    q0 = F_WIDTH // width
    k0 = q0 + NA_WIDTH // width
    v0 = k0 + NA_WIDTH // width
    prev = lambda b: jnp.maximum(b - 1, 0)
    nxt = lambda b: jnp.minimum(b + 1, nb - 1)
    variant = lambda b: jnp.where(b == 0, 0, jnp.where(b == nb - 1, 2, 1))
    spec = lambda col0, rowfn: pl.BlockSpec((blk, width), lambda h, b: (rowfn(b), col0 + h))
    same = lambda b: b
    return pl.pallas_call(
        _natten_kernel,
        grid=(NA_HEADS // NA_HEADS_PER_STEP, nb),
        in_specs=[
            spec(q0, same),
            spec(k0, prev), spec(k0, same), spec(k0, nxt),
            spec(v0, prev), spec(v0, same), spec(v0, nxt),
            pl.BlockSpec((1, NA_HEADS_PER_STEP, blk, NA_KROWS * GRID_W), lambda h, b: (variant(b), h, 0, 0)),
        ],
        out_specs=pl.BlockSpec((blk, width), lambda h, b: (b, h)),
        out_shape=jax.ShapeDtypeStruct((s, NA_WIDTH), jnp.bfloat16),
        compiler_params=_cparams("parallel", "arbitrary"),
        name="natten",
    )(z, z, z, z, z, z, z, bias)


def _layer_norm(h, g, b):
    mu = jnp.mean(h, axis=-1, keepdims=True)
    d = h - mu
    var = jnp.mean(d * d, axis=-1, keepdims=True)
    return d * lax.rsqrt(var + LN_EPS) * g + b


def _mix_ln1_kernel(x_ref, fm_ref, at_ref, ga_ref, gb_ref, fw_ref, nw_ref, wo_ref, g_ref, b_ref,
                    x1_ref, x1b_ref):
    a = jnp.dot(fm_ref[...], fw_ref[...], preferred_element_type=jnp.float32)
    b = jnp.dot(at_ref[...], nw_ref[...], preferred_element_type=jnp.float32)
    merged = ga_ref[...].astype(jnp.float32) * a + gb_ref[...].astype(jnp.float32) * b
    h = DN_ALPHA * x_ref[...] + jnp.dot(merged.astype(jnp.bfloat16), wo_ref[...],
                                        preferred_element_type=jnp.float32)
    x1 = _layer_norm(h, g_ref[...], b_ref[...])
    x1_ref[...] = x1
    x1b_ref[...] = x1.astype(jnp.bfloat16)


def _resident(shape):
    return pl.BlockSpec(shape, lambda *_: (0,) * len(shape), pipeline_mode=pl.Buffered(1))


def _mix_ln1(x, fmix, att, gates, fw_b, nw_b, wo_b, ln_g, ln_b, tm=256):
    s = x.shape[0]
    row = lambda width: pl.BlockSpec((tm, width), lambda i: (i, 0))
    return pl.pallas_call(
        _mix_ln1_kernel,
        grid=(s // tm,),
        in_specs=[
            row(D_MODEL), row(F_WIDTH), row(NA_WIDTH),
            pl.BlockSpec((tm, D_MODEL), lambda i: (i, 0)),
            pl.BlockSpec((tm, D_MODEL), lambda i: (i, 1)),
            _resident((F_WIDTH, D_MODEL)), _resident((NA_WIDTH, D_MODEL)), _resident((D_MODEL, D_MODEL)),
            _resident((1, D_MODEL)), _resident((1, D_MODEL)),
        ],
        out_specs=[row(D_MODEL), row(D_MODEL)],
        out_shape=[jax.ShapeDtypeStruct((s, D_MODEL), jnp.float32),
                   jax.ShapeDtypeStruct((s, D_MODEL), jnp.bfloat16)],
        compiler_params=_cparams("parallel"),
        name="mix_ln1",
    )(x, fmix, att, gates, gates, fw_b, nw_b, wo_b, ln_g, ln_b)


FFN_HALO = 16


def _gelu_tanh(u):
    return 0.5 * u * (1.0 + jnp.tanh(math.sqrt(2.0 / math.pi) * (u + 0.044715 * (u * u * u))))


LANES = 128
FFN_ROW_BLOCK = 128


def _ffn_up_kernel(xp_ref, xc_ref, xn_ref, wa_ref, wb_ref, ca_ref, cb_ref, ba_ref, bb_ref, h_ref,
                   xe_ref, ua_ref, ub_ref, *, tm, tn, n_row_tiles):
    i = pl.program_id(1)
    xe_ref[:FFN_HALO, :] = jnp.where(i > 0, xp_ref[...], jnp.zeros_like(xp_ref))
    xe_ref[FFN_HALO:FFN_HALO + tm, :] = xc_ref[...]
    xe_ref[FFN_HALO + tm:, :] = jnp.where(i < n_row_tiles - 1, xn_ref[...], jnp.zeros_like(xn_ref))
    xe = xe_ref[...]
    lanes_per_chunk = MXU_COLS // LANES

    def matmul_chunk(c, slot):
        sl = slice(c * MXU_COLS, (c + 1) * MXU_COLS)
        for w_ref, u_ref in ((wa_ref, ua_ref), (wb_ref, ub_ref)):
            u = jnp.dot(xe, w_ref[:, sl], preferred_element_type=jnp.float32)
            for k in range(lanes_per_chunk):
                u_ref[slot, k] = u[:, k * LANES:(k + 1) * LANES]

    def epilogue_chunk(c, slot):
        for k in range(lanes_per_chunk):
            sl = slice(c * MXU_COLS + k * LANES, c * MXU_COLS + (k + 1) * LANES)
            taps = []
            for c_ref, b_ref in ((ca_ref, ba_ref), (cb_ref, bb_ref)):
                taps.append([jnp.broadcast_to(c_ref[t:t + 1, sl], (FFN_ROW_BLOCK, LANES)) for t in range(3)]
                            + [jnp.broadcast_to(b_ref[:, sl], (FFN_ROW_BLOCK, LANES))])
            for r in range(0, tm, FFN_ROW_BLOCK):
                halves = []
                for u_ref, (c0, c1, c2, bias) in zip((ua_ref, ub_ref), taps):
                    lo = u_ref[slot, k, pl.ds(FFN_HALO - 1 + r, FFN_ROW_BLOCK), :]
                    mid = u_ref[slot, k, pl.ds(FFN_HALO + r, FFN_ROW_BLOCK), :]
                    hi = u_ref[slot, k, pl.ds(FFN_HALO + 1 + r, FFN_ROW_BLOCK), :]
                    halves.append(lo * c0 + mid * c1 + hi * c2 + bias)
                h_ref[pl.ds(r, FFN_ROW_BLOCK), sl] = (_gelu_tanh(halves[0]) * halves[1]).astype(h_ref.dtype)

    n_chunks = tn // MXU_COLS
    for c in range(n_chunks):
        matmul_chunk(c, c % 2)
        if c > 0:
            epilogue_chunk(c - 1, (c - 1) % 2)
    epilogue_chunk(n_chunks - 1, (n_chunks - 1) % 2)


def _ffn_up(x1b, up_b, conv, conv_b, tm=1024, tn=D_FF // 2):
    s = x1b.shape[0]
    n_row_tiles = s // tm
    n_col = D_FF // tn
    halo_per_tile = tm // FFN_HALO
    n_halo_blocks = s // FFN_HALO
    prev = lambda j, i: (jnp.maximum(i * halo_per_tile - 1, 0), 0)
    nxt = lambda j, i: (jnp.minimum((i + 1) * halo_per_tile, n_halo_blocks - 1), 0)
    wspec = lambda shape, off: pl.BlockSpec(shape, lambda j, i: (0, j + off), pipeline_mode=pl.Buffered(1))
    return pl.pallas_call(
        functools.partial(_ffn_up_kernel, tm=tm, tn=tn, n_row_tiles=n_row_tiles),
        grid=(n_col, n_row_tiles),
        in_specs=[
            pl.BlockSpec((FFN_HALO, D_MODEL), prev),
            pl.BlockSpec((tm, D_MODEL), lambda j, i: (i, 0)),
            pl.BlockSpec((FFN_HALO, D_MODEL), nxt),
            wspec((D_MODEL, tn), 0), wspec((D_MODEL, tn), n_col),
            wspec((3, tn), 0), wspec((3, tn), n_col),
            wspec((1, tn), 0), wspec((1, tn), n_col),
        ],
        out_specs=pl.BlockSpec((tm, tn), lambda j, i: (i, j)),
        out_shape=jax.ShapeDtypeStruct((s, D_FF), jnp.bfloat16),
        scratch_shapes=[pltpu.VMEM((tm + 2 * FFN_HALO, D_MODEL), jnp.bfloat16),
                        pltpu.VMEM((2, MXU_COLS // LANES, tm + 2 * FFN_HALO, LANES), jnp.float32),
                        pltpu.VMEM((2, MXU_COLS // LANES, tm + 2 * FFN_HALO, LANES), jnp.float32)],
        compiler_params=_cparams("arbitrary", "arbitrary"),
        name="ffn_up",
    )(x1b, x1b, x1b, up_b, up_b, conv, conv, conv_b, conv_b)


def _ffn_down_ln2_kernel(x1_ref, x1b_ref, p_ref, h_ref, wd_ref, pg_ref, pp_ref, g_ref, b_ref, o_ref, acc_ref,
                         *, n_k):
    k = pl.program_id(1)

    @pl.when(k == 0)
    def _():
        gate = jax.nn.sigmoid(jnp.dot(x1b_ref[...], pg_ref[...], preferred_element_type=jnp.float32))
        emb = jnp.dot(p_ref[...].astype(jnp.bfloat16), pp_ref[...], preferred_element_type=jnp.float32)
        acc_ref[...] = DN_ALPHA * x1_ref[...] + gate * emb

    acc_ref[...] += jnp.dot(h_ref[...], wd_ref[...], preferred_element_type=jnp.float32)

    @pl.when(k == n_k - 1)
    def _():
        o_ref[...] = _layer_norm(acc_ref[...], g_ref[...], b_ref[...])


def _ffn_down_ln2(x1, x1b, p, h, wd_b, pg_b, pp_b, ln_g, ln_b, tm=512, tk=1408):
    s = x1.shape[0]
    n_k = D_FF // tk
    return pl.pallas_call(
        functools.partial(_ffn_down_ln2_kernel, n_k=n_k),
        grid=(s // tm, n_k),
        in_specs=[
            pl.BlockSpec((tm, D_MODEL), lambda i, k: (i, 0)),
            pl.BlockSpec((tm, D_MODEL), lambda i, k: (i, 0)),
            pl.BlockSpec((tm, PLE_DIM), lambda i, k: (i, 0)),
            pl.BlockSpec((tm, tk), lambda i, k: (i, k)),
            pl.BlockSpec((tk, D_MODEL), lambda i, k: (k, 0)),
            _resident((D_MODEL, D_MODEL)), _resident((PLE_DIM, D_MODEL)),
            _resident((1, D_MODEL)), _resident((1, D_MODEL)),
        ],
        out_specs=pl.BlockSpec((tm, D_MODEL), lambda i, k: (i, 0)),
        out_shape=jax.ShapeDtypeStruct((s, D_MODEL), jnp.float32),
        scratch_shapes=[pltpu.VMEM((tm, D_MODEL), jnp.float32)],
        compiler_params=_cparams("parallel", "arbitrary"),
        name="ffn_down_ln2",
    )(x1, x1b, p, h, wd_b, pg_b, pp_b, ln_g, ln_b)


def _encoder_layer(x, p, wts, bias, tables):
    z, gates = _in_proj(x, wts["w_in"], wts["gate_b"])
    fmix = _fourier_mix(z, tables)
    att = _natten(z, bias)
    x1, x1b = _mix_ln1(x, fmix, att, gates, wts["fourier_w"], wts["natten_w"], wts["w_out"],
                       wts["ln1_g"], wts["ln1_b"])
    h = _ffn_up(x1b, wts["ffn_up"], wts["ffn_conv"], wts["ffn_conv_b"])
    return _ffn_down_ln2(x1, x1b, p, h, wts["ffn_down"], wts["ple_gate"], wts["ple_proj"],
                         wts["ln2_g"], wts["ln2_b"])


def kernel(x_prompt, x_sample, p_prompt, p_sample, w_in, gate_b, fourier_w, natten_rpb, natten_w, w_out,
           ln1_g, ln1_b, ffn_up, ffn_conv, ffn_conv_b, ffn_down, ple_proj, ple_gate, ln2_g, ln2_b):
    assert w_in.shape[0] == DEPTH
    xp, xs = x_prompt[0], x_sample[0]
    bf = lambda w: w.astype(jnp.bfloat16)
    row = lambda v: v.reshape(1, -1)
    for i in range(DEPTH):
        wts = {
            "w_in": bf(w_in[i]), "gate_b": row(gate_b[i]),
            "fourier_w": bf(fourier_w[i]), "natten_w": bf(natten_w[i]), "w_out": bf(w_out[i]),
            "ln1_g": row(ln1_g[i]), "ln1_b": row(ln1_b[i]),
            "ffn_up": bf(ffn_up[i]), "ffn_conv": ffn_conv[i], "ffn_conv_b": row(ffn_conv_b[i]),
            "ffn_down": bf(ffn_down[i]), "ple_proj": bf(ple_proj[i]), "ple_gate": bf(ple_gate[i]),
            "ln2_g": row(ln2_g[i]), "ln2_b": row(ln2_b[i]),
        }
        bias = _natten_bias(natten_rpb[i])
        xp = _encoder_layer(xp, p_prompt[i, 0], wts, bias, _dft_tables(xp.shape[0]))
        xs = _encoder_layer(xs, p_sample[i, 0], wts, bias, _dft_tables(xs.shape[0]))
    return (xp[None], xs[None])
```

```python
import functools
import math

import jax
import jax.numpy as jnp
import numpy as np
from jax import lax
from jax.experimental import pallas as pl
from jax.experimental.pallas import tpu as pltpu

D_MODEL = 2048
GRID_W = 64
PLE_DIM = 256
F_WIDTH = 1024
F_GROUPS = 8
F_GROUP_CH = F_WIDTH // F_GROUPS
NA_HEADS = 8
NA_HEAD_DIM = 128
NA_WIDTH = NA_HEADS * NA_HEAD_DIM
WIN_H = 8
WIN_W = 16
D_FF = 5632
IN_COLS = F_WIDTH + 3 * NA_WIDTH + 2 * D_MODEL
LN_EPS = 1e-5
DEPTH = 1
DN_ALPHA = (2.0 * DEPTH) ** 0.25

VMEM_LIMIT_BYTES = 56 * 1024 * 1024
NEG_BIAS = -1e30

DFT_N2 = 128
NA_QROWS = 8
NA_KROWS = 16
NA_KROW_OFFSET = 4


def _cparams(*semantics):
    return pltpu.CompilerParams(dimension_semantics=semantics, vmem_limit_bytes=VMEM_LIMIT_BYTES)


MXU_COLS = 256


def _in_proj_kernel(x_ref, wz_ref, wg_ref, gb_ref, z_ref, g_ref, xb_ref, *, tn):
    @pl.when(pl.program_id(1) == 0)
    def _():
        xb_ref[...] = x_ref[...].astype(jnp.bfloat16)

    xb = xb_ref[...]
    for c in range(0, tn, MXU_COLS):
        sl = slice(c, c + MXU_COLS)
        acc = jnp.dot(xb, wg_ref[:, sl], preferred_element_type=jnp.float32)
        g_ref[:, sl] = jax.nn.sigmoid(acc + gb_ref[:, sl]).astype(g_ref.dtype)
    for c in range(0, tn, MXU_COLS):
        sl = slice(c, c + MXU_COLS)
        z_ref[:, sl] = jnp.dot(xb, wz_ref[:, sl], preferred_element_type=jnp.float32).astype(z_ref.dtype)


def _in_proj(x, w_in_b, gate_b, tm=1024, tn=1024):
    s = x.shape[0]
    z_cols = F_WIDTH + 3 * NA_WIDTH
    g_cols = IN_COLS - z_cols
    assert z_cols == g_cols
    n_col = z_cols // tn
    return pl.pallas_call(
        functools.partial(_in_proj_kernel, tn=tn),
        grid=(s // tm, n_col),
        in_specs=[
            pl.BlockSpec((tm, D_MODEL), lambda i, j: (i, 0)),
            pl.BlockSpec((D_MODEL, tn), lambda i, j: (0, j)),
            pl.BlockSpec((D_MODEL, tn), lambda i, j: (0, j + n_col)),
            pl.BlockSpec((1, tn), lambda i, j: (0, j)),
        ],
        out_specs=[pl.BlockSpec((tm, tn), lambda i, j: (i, j)),
                   pl.BlockSpec((tm, tn), lambda i, j: (i, j))],
        out_shape=[jax.ShapeDtypeStruct((s, z_cols), jnp.bfloat16),
                   jax.ShapeDtypeStruct((s, g_cols), jnp.bfloat16)],
        scratch_shapes=[pltpu.VMEM((tm, D_MODEL), jnp.bfloat16)],
        compiler_params=_cparams("parallel", "arbitrary"),
        name="in_proj",
    )(x, w_in_b, w_in_b, gate_b)


def _dft_tables(s):
    n2 = DFT_N2
    n1 = s // n2
    k1 = np.arange(n1)
    w1 = np.exp(-2j * np.pi * np.outer(k1, k1) / n1) / math.sqrt(n1)
    fa = np.concatenate([w1.real, w1.imag], axis=0)
    k2 = np.arange(n2)
    w2 = np.exp(-2j * np.pi * np.outer(k2, k2) / n2) / math.sqrt(n2)
    tw = np.exp(-2j * np.pi * np.outer(k1, k2) / s)
    c = np.arange(F_GROUP_CH)
    wc = np.exp(-2j * np.pi * np.outer(c, c) / F_GROUP_CH) / math.sqrt(F_GROUP_CH)
    wch = np.concatenate([wc.real, -wc.imag], axis=0)
    f32 = lambda a: jnp.asarray(a, dtype=jnp.float32)
    return (f32(fa), f32(w2.real), f32(w2.imag), f32(tw.real), f32(tw.imag), f32(wch))


DFT_SLABS = 16


def _dft_pos_a_kernel(fa_ref, x_ref, o_ref, *, n1):
    fa = fa_ref[...].astype(jnp.bfloat16)
    xt = jnp.swapaxes(x_ref[...], 0, 1)
    re, im = [], []
    for j in range(DFT_SLABS):
        p = jnp.dot(fa, xt[j], preferred_element_type=jnp.float32)
        re.append(p[:n1].astype(o_ref.dtype))
        im.append(p[n1:].astype(o_ref.dtype))
    o_ref[0] = jnp.swapaxes(jnp.stack(re, axis=0), 0, 1)
    o_ref[1] = jnp.swapaxes(jnp.stack(im, axis=0), 0, 1)


def _dft_pos_a(z3, fa):
    n1, n2, _ = z3.shape
    return pl.pallas_call(
        functools.partial(_dft_pos_a_kernel, n1=n1),
        grid=(n2 // DFT_SLABS,),
        in_specs=[
            pl.BlockSpec((2 * n1, n1), lambda i: (0, 0)),
            pl.BlockSpec((n1, DFT_SLABS, F_WIDTH), lambda i: (0, i, 0)),
        ],
        out_specs=pl.BlockSpec((2, n1, DFT_SLABS, F_WIDTH), lambda i: (0, 0, i, 0)),
        out_shape=jax.ShapeDtypeStruct((2, n1, n2, F_WIDTH), jnp.bfloat16),
        compiler_params=_cparams("parallel"),
        name="dft_pos_a",
    )(fa, z3)


def _dft_pos_c_kernel(w2r_ref, w2i_ref, twr_ref, twi_ref, wch_ref, a_ref, o_ref):
    w2r = w2r_ref[...]
    w2i = w2i_ref[...]
    wch = wch_ref[...].astype(jnp.bfloat16)
    slabs = []
    for j in range(DFT_SLABS):
        tr = twr_ref[j:j + 1, :]
        ti = twi_ref[j:j + 1, :]
        gr = (w2r * tr - w2i * ti).astype(jnp.bfloat16)
        gi = (w2r * ti + w2i * tr).astype(jnp.bfloat16)
        g4 = jnp.concatenate(
            [jnp.concatenate([gr, -gi], axis=1), jnp.concatenate([gi, gr], axis=1)], axis=0)
        a = jnp.concatenate([a_ref[0, j], a_ref[1, j]], axis=0)
        y = jnp.dot(g4, a, preferred_element_type=jnp.float32).astype(jnp.bfloat16)
        n2 = y.shape[0] // 2
        yr, yi = y[:n2], y[n2:]
        groups = []
        for g in range(F_GROUPS):
            sl = slice(g * F_GROUP_CH, (g + 1) * F_GROUP_CH)
            lhs = jnp.concatenate([yr[:, sl], yi[:, sl]], axis=1)
            groups.append(jnp.dot(lhs, wch, preferred_element_type=jnp.float32).astype(o_ref.dtype))
        slabs.append(jnp.concatenate(groups, axis=1))
    o_ref[...] = jnp.swapaxes(jnp.stack(slabs, axis=0), 0, 1)


def _dft_pos_c(a, w2r, w2i, twr, twi, wch):
    _, n1, n2, c = a.shape
    full = lambda shape: pl.BlockSpec(shape, lambda i: (0,) * len(shape))
    return pl.pallas_call(
        _dft_pos_c_kernel,
        grid=(n1 // DFT_SLABS,),
        in_specs=[
            full((n2, n2)), full((n2, n2)),
            pl.BlockSpec((DFT_SLABS, n2), lambda i: (i, 0)),
            pl.BlockSpec((DFT_SLABS, n2), lambda i: (i, 0)),
            full((2 * F_GROUP_CH, F_GROUP_CH)),
            pl.BlockSpec((2, DFT_SLABS, n2, c), lambda i: (0, i, 0, 0)),
        ],
        out_specs=pl.BlockSpec((n2, DFT_SLABS, c), lambda i: (0, i, 0)),
        out_shape=jax.ShapeDtypeStruct((n2, n1, c), jnp.bfloat16),
        compiler_params=_cparams("parallel"),
        name="dft_pos_c",
    )(w2r, w2i, twr, twi, wch, a)


def _fourier_mix(z, tables):
    s = z.shape[0]
    n2 = DFT_N2
    n1 = s // n2
    fa, w2r, w2i, twr, twi, wch = tables
    a = _dft_pos_a(z.reshape(n1, n2, z.shape[1]), fa)
    y = _dft_pos_c(a, w2r, w2i, twr, twi, wch)
    return y.reshape(s, F_WIDTH)


NA_RPB_H = 2 * WIN_H - 1
NA_RPB_W = 2 * WIN_W - 1
NA_KROW_PAIRS = NA_KROWS // 2


def _natten_row_windows():
    i = np.arange(NA_QROWS)
    mid = np.full(NA_QROWS, -(WIN_H // 2))
    first = np.where(i < WIN_H // 2, -i, mid)
    last = np.where(i > WIN_H // 2, -i, mid)
    return (first, mid, last)


def _natten_bias_kernel(rpb_ref, o_ref, cb_ref):
    h = pl.program_id(0)
    shape = (GRID_W, 2 * GRID_W)
    qc = lax.broadcasted_iota(jnp.int32, shape, 0)
    lane = lax.broadcasted_iota(jnp.int32, shape, 1)
    kc = lane & (GRID_W - 1)
    col_start = jnp.clip(qc - WIN_W // 2, 0, GRID_W - WIN_W)
    col_ok = (kc >= col_start) & (kc < col_start + WIN_W)
    dc = jnp.where(col_ok, kc - qc + WIN_W - 1, -1)
    neg = jnp.full(shape, NEG_BIAS, jnp.float32)
    for d in range(NA_RPB_H):
        tile = neg
        for t in range(NA_RPB_W):
            r = rpb_ref[(h * NA_RPB_H + d) * NA_RPB_W + t] * (NA_HEAD_DIM ** 0.5)
            tile = jnp.where(dc == t, r, tile)
        cb_ref[d] = tile
    left = lane < GRID_W
    for v, lo in enumerate(_natten_row_windows()):
        for i in range(NA_QROWS):
            for jj in range(NA_KROW_PAIRS):
                dr = (2 * jj - NA_KROW_OFFSET - i, 2 * jj + 1 - NA_KROW_OFFSET - i)
                ok = [lo[i] <= d <= lo[i] + WIN_H - 1 for d in dr]
                halves = [cb_ref[d + WIN_H - 1] if good else neg for d, good in zip(dr, ok)]
                tile = jnp.where(left, halves[0], halves[1]) if any(ok) else neg
                o_ref[v, 0, i * GRID_W:(i + 1) * GRID_W, jj * 2 * GRID_W:(jj + 1) * 2 * GRID_W] = tile


def _natten_bias(rpb):
    blk = NA_QROWS * GRID_W
    keys = NA_KROWS * GRID_W
    return pl.pallas_call(
        _natten_bias_kernel,
        grid=(NA_HEADS,),
        in_specs=[pl.BlockSpec(memory_space=pltpu.SMEM)],
        out_specs=pl.BlockSpec((3, 1, blk, keys), lambda h: (0, h, 0, 0)),
        out_shape=jax.ShapeDtypeStruct((3, NA_HEADS, blk, keys), jnp.float32),
        scratch_shapes=[pltpu.VMEM((NA_RPB_H, GRID_W, 2 * GRID_W), jnp.float32)],
        compiler_params=_cparams("parallel"),
        name="natten_bias",
    )(rpb.astype(jnp.float32).reshape(-1))


NA_HEADS_PER_STEP = 4


def _natten_kernel(q_ref, kp_ref, kc_ref, kn_ref, vp_ref, vc_ref, vn_ref, b_ref, o_ref):
    half = (NA_KROWS - NA_QROWS) // 2 * GRID_W
    blk = NA_QROWS * GRID_W
    exp2_scale = (NA_HEAD_DIM ** -0.5) * math.log2(math.e)
    for h in range(NA_HEADS_PER_STEP):
        sl = slice(h * NA_HEAD_DIM, (h + 1) * NA_HEAD_DIM)
        windows = (
            (slice(0, blk // 2), slice(0, blk + half), (kp_ref, kc_ref), (vp_ref, vc_ref),
             (slice(blk - half, blk), slice(None))),
            (slice(blk // 2, blk), slice(half, blk + 2 * half), (kc_ref, kn_ref), (vc_ref, vn_ref),
             (slice(None), slice(0, half))),
        )
        for rows, keys, k_refs, v_refs, parts in windows:
            kwin = jnp.concatenate([r[p, sl] for r, p in zip(k_refs, parts)], axis=0)
            vwin = jnp.concatenate([r[p, sl] for r, p in zip(v_refs, parts)], axis=0)
            t = lax.dot_general(q_ref[rows, sl], kwin, (((1,), (1,)), ((), ())),
                                preferred_element_type=jnp.float32) + b_ref[0, h, rows, keys]
            m = jnp.max(t, axis=-1, keepdims=True)
            p = jnp.exp2((t - m) * exp2_scale)
            l = jnp.sum(p, axis=-1, keepdims=True)
            o = jnp.dot(p.astype(jnp.bfloat16), vwin, preferred_element_type=jnp.float32)
            o_ref[rows, sl] = (o / l).astype(o_ref.dtype)


def _natten(z, bias):
    s = z.shape[0]
    blk = NA_QROWS * GRID_W
    nb = s // blk
    width = NA_HEADS_PER_STEP * NA_HEAD_DIM
    q0 = F_WIDTH // width
    k0 = q0 + NA_WIDTH // width
    v0 = k0 + NA_WIDTH // width
    prev = lambda b: jnp.maximum(b - 1, 0)
    nxt = lambda b: jnp.minimum(b + 1, nb - 1)
    variant = lambda b: jnp.where(b == 0, 0, jnp.where(b == nb - 1, 2, 1))
    spec = lambda col0, rowfn: pl.BlockSpec((blk, width), lambda h, b: (rowfn(b), col0 + h))
    same = lambda b: b
    return pl.pallas_call(
        _natten_kernel,
        grid=(NA_HEADS // NA_HEADS_PER_STEP, nb),
        in_specs=[
            spec(q0, same),
            spec(k0, prev), spec(k0, same), spec(k0, nxt),
            spec(v0, prev), spec(v0, same), spec(v0, nxt),
            pl.BlockSpec((1, NA_HEADS_PER_STEP, blk, NA_KROWS * GRID_W), lambda h, b: (variant(b), h, 0, 0)),
        ],
        out_specs=pl.BlockSpec((blk, width), lambda h, b: (b, h)),
        out_shape=jax.ShapeDtypeStruct((s, NA_WIDTH), jnp.bfloat16),
        compiler_params=_cparams("parallel", "arbitrary"),
        name="natten",
    )(z, z, z, z, z, z, z, bias)


def _layer_norm(h, g, b):
    mu = jnp.mean(h, axis=-1, keepdims=True)
    d = h - mu
    var = jnp.mean(d * d, axis=-1, keepdims=True)
    return d * lax.rsqrt(var + LN_EPS) * g + b


def _mix_ln1_kernel(x_ref, fm_ref, at_ref, ga_ref, gb_ref, fw_ref, nw_ref, wo_ref, g_ref, b_ref,
                    x1_ref, x1b_ref):
    a = jnp.dot(fm_ref[...], fw_ref[...], preferred_element_type=jnp.float32)
    b = jnp.dot(at_ref[...], nw_ref[...], preferred_element_type=jnp.float32)
    merged = ga_ref[...].astype(jnp.float32) * a + gb_ref[...].astype(jnp.float32) * b
    h = DN_ALPHA * x_ref[...] + jnp.dot(merged.astype(jnp.bfloat16), wo_ref[...],
                                        preferred_element_type=jnp.float32)
    x1 = _layer_norm(h, g_ref[...], b_ref[...])
    x1_ref[...] = x1
    x1b_ref[...] = x1.astype(jnp.bfloat16)


def _resident(shape):
    return pl.BlockSpec(shape, lambda *_: (0,) * len(shape), pipeline_mode=pl.Buffered(1))


def _mix_ln1(x, fmix, att, gates, fw_b, nw_b, wo_b, ln_g, ln_b, tm=256):
    s = x.shape[0]
    row = lambda width: pl.BlockSpec((tm, width), lambda i: (i, 0))
    return pl.pallas_call(
        _mix_ln1_kernel,
        grid=(s // tm,),
        in_specs=[
            row(D_MODEL), row(F_WIDTH), row(NA_WIDTH),
            pl.BlockSpec((tm, D_MODEL), lambda i: (i, 0)),
            pl.BlockSpec((tm, D_MODEL), lambda i: (i, 1)),
            _resident((F_WIDTH, D_MODEL)), _resident((NA_WIDTH, D_MODEL)), _resident((D_MODEL, D_MODEL)),
            _resident((1, D_MODEL)), _resident((1, D_MODEL)),
        ],
        out_specs=[row(D_MODEL), row(D_MODEL)],
        out_shape=[jax.ShapeDtypeStruct((s, D_MODEL), jnp.float32),
                   jax.ShapeDtypeStruct((s, D_MODEL), jnp.bfloat16)],
        compiler_params=_cparams("parallel"),
        name="mix_ln1",
    )(x, fmix, att, gates, gates, fw_b, nw_b, wo_b, ln_g, ln_b)


FFN_HALO = 16


def _gelu_tanh(u):
    return 0.5 * u * (1.0 + jnp.tanh(math.sqrt(2.0 / math.pi) * (u + 0.044715 * (u * u * u))))


LANES = 128
FFN_ROW_BLOCK = 128


def _ffn_up_kernel(xp_ref, xc_ref, xn_ref, wa_ref, wb_ref, ca_ref, cb_ref, ba_ref, bb_ref, h_ref,
                   xe_ref, ua_ref, ub_ref, *, tm, tn, n_row_tiles):
    i = pl.program_id(1)
    xe_ref[:FFN_HALO, :] = jnp.where(i > 0, xp_ref[...], jnp.zeros_like(xp_ref))
    xe_ref[FFN_HALO:FFN_HALO + tm, :] = xc_ref[...]
    xe_ref[FFN_HALO + tm:, :] = jnp.where(i < n_row_tiles - 1, xn_ref[...], jnp.zeros_like(xn_ref))
    xe = xe_ref[...]
    lanes_per_chunk = MXU_COLS // LANES

    def matmul_chunk(c, slot):
        sl = slice(c * MXU_COLS, (c + 1) * MXU_COLS)
        for w_ref, u_ref in ((wa_ref, ua_ref), (wb_ref, ub_ref)):
            u = jnp.dot(xe, w_ref[:, sl], preferred_element_type=jnp.float32)
            for k in range(lanes_per_chunk):
                u_ref[slot, k] = u[:, k * LANES:(k + 1) * LANES]

    def epilogue_chunk(c, slot):
        for k in range(lanes_per_chunk):
            sl = slice(c * MXU_COLS + k * LANES, c * MXU_COLS + (k + 1) * LANES)
            taps = []
            for c_ref, b_ref in ((ca_ref, ba_ref), (cb_ref, bb_ref)):
                taps.append([jnp.broadcast_to(c_ref[t:t + 1, sl], (FFN_ROW_BLOCK, LANES)) for t in range(3)]
                            + [jnp.broadcast_to(b_ref[:, sl], (FFN_ROW_BLOCK, LANES))])
            for r in range(0, tm, FFN_ROW_BLOCK):
                halves = []
                for u_ref, (c0, c1, c2, bias) in zip((ua_ref, ub_ref), taps):
                    lo = u_ref[slot, k, pl.ds(FFN_HALO - 1 + r, FFN_ROW_BLOCK), :]
                    mid = u_ref[slot, k, pl.ds(FFN_HALO + r, FFN_ROW_BLOCK), :]
                    hi = u_ref[slot, k, pl.ds(FFN_HALO + 1 + r, FFN_ROW_BLOCK), :]
                    halves.append(lo * c0 + mid * c1 + hi * c2 + bias)
                h_ref[pl.ds(r, FFN_ROW_BLOCK), sl] = (_gelu_tanh(halves[0]) * halves[1]).astype(h_ref.dtype)

    n_chunks = tn // MXU_COLS
    for c in range(n_chunks):
        matmul_chunk(c, c % 2)
        if c > 0:
            epilogue_chunk(c - 1, (c - 1) % 2)
    epilogue_chunk(n_chunks - 1, (n_chunks - 1) % 2)


def _ffn_up(x1b, up_b, conv, conv_b, tm=1024, tn=D_FF // 2):
    s = x1b.shape[0]
    n_row_tiles = s // tm
    n_col = D_FF // tn
    halo_per_tile = tm // FFN_HALO
    n_halo_blocks = s // FFN_HALO
    prev = lambda j, i: (jnp.maximum(i * halo_per_tile - 1, 0), 0)
    nxt = lambda j, i: (jnp.minimum((i + 1) * halo_per_tile, n_halo_blocks - 1), 0)
    wspec = lambda shape, off: pl.BlockSpec(shape, lambda j, i: (0, j + off), pipeline_mode=pl.Buffered(1))
    return pl.pallas_call(
        functools.partial(_ffn_up_kernel, tm=tm, tn=tn, n_row_tiles=n_row_tiles),
        grid=(n_col, n_row_tiles),
        in_specs=[
            pl.BlockSpec((FFN_HALO, D_MODEL), prev),
            pl.BlockSpec((tm, D_MODEL), lambda j, i: (i, 0)),
            pl.BlockSpec((FFN_HALO, D_MODEL), nxt),
            wspec((D_MODEL, tn), 0), wspec((D_MODEL, tn), n_col),
            wspec((3, tn), 0), wspec((3, tn), n_col),
            wspec((1, tn), 0), wspec((1, tn), n_col),
        ],
        out_specs=pl.BlockSpec((tm, tn), lambda j, i: (i, j)),
        out_shape=jax.ShapeDtypeStruct((s, D_FF), jnp.bfloat16),
        scratch_shapes=[pltpu.VMEM((tm + 2 * FFN_HALO, D_MODEL), jnp.bfloat16),
                        pltpu.VMEM((2, MXU_COLS // LANES, tm + 2 * FFN_HALO, LANES), jnp.float32),
                        pltpu.VMEM((2, MXU_COLS // LANES, tm + 2 * FFN_HALO, LANES), jnp.float32)],
        compiler_params=_cparams("arbitrary", "arbitrary"),
        name="ffn_up",
    )(x1b, x1b, x1b, up_b, up_b, conv, conv, conv_b, conv_b)


def _ffn_down_ln2_kernel(x1_ref, x1b_ref, p_ref, h_ref, wd_ref, pg_ref, pp_ref, g_ref, b_ref, o_ref, acc_ref,
                         *, n_k):
    k = pl.program_id(1)

    @pl.when(k == 0)
    def _():
        gate = jax.nn.sigmoid(jnp.dot(x1b_ref[...], pg_ref[...], preferred_element_type=jnp.float32))
        emb = jnp.dot(p_ref[...].astype(jnp.bfloat16), pp_ref[...], preferred_element_type=jnp.float32)
        acc_ref[...] = DN_ALPHA * x1_ref[...] + gate * emb

    acc_ref[...] += jnp.dot(h_ref[...], wd_ref[...], preferred_element_type=jnp.float32)

    @pl.when(k == n_k - 1)
    def _():
        o_ref[...] = _layer_norm(acc_ref[...], g_ref[...], b_ref[...])


def _ffn_down_ln2(x1, x1b, p, h, wd_b, pg_b, pp_b, ln_g, ln_b, tm=512, tk=1408):
    s = x1.shape[0]
    n_k = D_FF // tk
    return pl.pallas_call(
        functools.partial(_ffn_down_ln2_kernel, n_k=n_k),
        grid=(s // tm, n_k),
        in_specs=[
            pl.BlockSpec((tm, D_MODEL), lambda i, k: (i, 0)),
            pl.BlockSpec((tm, D_MODEL), lambda i, k: (i, 0)),
            pl.BlockSpec((tm, PLE_DIM), lambda i, k: (i, 0)),
            pl.BlockSpec((tm, tk), lambda i, k: (i, k)),
            pl.BlockSpec((tk, D_MODEL), lambda i, k: (k, 0)),
            _resident((D_MODEL, D_MODEL)), _resident((PLE_DIM, D_MODEL)),
            _resident((1, D_MODEL)), _resident((1, D_MODEL)),
        ],
        out_specs=pl.BlockSpec((tm, D_MODEL), lambda i, k: (i, 0)),
        out_shape=jax.ShapeDtypeStruct((s, D_MODEL), jnp.float32),
        scratch_shapes=[pltpu.VMEM((tm, D_MODEL), jnp.float32)],
        compiler_params=_cparams("parallel", "arbitrary"),
        name="ffn_down_ln2",
    )(x1, x1b, p, h, wd_b, pg_b, pp_b, ln_g, ln_b)


def _encoder_layer(x, p, wts, bias, tables):
    z, gates = _in_proj(x, wts["w_in"], wts["gate_b"])
    fmix = _fourier_mix(z, tables)
    att = _natten(z, bias)
    x1, x1b = _mix_ln1(x, fmix, att, gates, wts["fourier_w"], wts["natten_w"], wts["w_out"],
                       wts["ln1_g"], wts["ln1_b"])
    h = _ffn_up(x1b, wts["ffn_up"], wts["ffn_conv"], wts["ffn_conv_b"])
    return _ffn_down_ln2(x1, x1b, p, h, wts["ffn_down"], wts["ple_gate"], wts["ple_proj"],
                         wts["ln2_g"], wts["ln2_b"])


def kernel(x_prompt, x_sample, p_prompt, p_sample, w_in, gate_b, fourier_w, natten_rpb, natten_w, w_out,
           ln1_g, ln1_b, ffn_up, ffn_conv, ffn_conv_b, ffn_down, ple_proj, ple_gate, ln2_g, ln2_b):
    assert w_in.shape[0] == DEPTH
    xp, xs = x_prompt[0], x_sample[0]
    bf = lambda w: w.astype(jnp.bfloat16)
    row = lambda v: v.reshape(1, -1)
    for i in range(DEPTH):
        wts = {
            "w_in": bf(w_in[i]), "gate_b": row(gate_b[i]),
            "fourier_w": bf(fourier_w[i]), "natten_w": bf(natten_w[i]), "w_out": bf(w_out[i]),
            "ln1_g": row(ln1_g[i]), "ln1_b": row(ln1_b[i]),
            "ffn_up": bf(ffn_up[i]), "ffn_conv": ffn_conv[i], "ffn_conv_b": row(ffn_conv_b[i]),
            "ffn_down": bf(ffn_down[i]), "ple_proj": bf(ple_proj[i]), "ple_gate": bf(ple_gate[i]),
            "ln2_g": row(ln2_g[i]), "ln2_b": row(ln2_b[i]),
        }
        bias = _natten_bias(natten_rpb[i])
        xp = _encoder_layer(xp, p_prompt[i, 0], wts, bias, _dft_tables(xp.shape[0]))
        xs = _encoder_layer(xs, p_sample[i, 0], wts, bias, _dft_tables(xs.shape[0]))
    return (xp[None], xs[None])
```

```python
import functools
import math

import jax
import jax.numpy as jnp
import numpy as np
from jax import lax
from jax.experimental import pallas as pl
from jax.experimental.pallas import tpu as pltpu

D_MODEL = 2048
GRID_W = 64
PLE_DIM = 256
F_WIDTH = 1024
F_GROUPS = 8
F_GROUP_CH = F_WIDTH // F_GROUPS
NA_HEADS = 8
NA_HEAD_DIM = 128
NA_WIDTH = NA_HEADS * NA_HEAD_DIM
WIN_H = 8
WIN_W = 16
D_FF = 5632
IN_COLS = F_WIDTH + 3 * NA_WIDTH + 2 * D_MODEL
LN_EPS = 1e-5
DEPTH = 1
DN_ALPHA = (2.0 * DEPTH) ** 0.25

VMEM_LIMIT_BYTES = 56 * 1024 * 1024
NEG_BIAS = -1e30

DFT_N2 = 128
NA_QROWS = 8
NA_KROWS = 16
NA_KROW_OFFSET = 4


def _cparams(*semantics):
    return pltpu.CompilerParams(dimension_semantics=semantics, vmem_limit_bytes=VMEM_LIMIT_BYTES)


MXU_COLS = 256


def _in_proj_kernel(x_ref, wz_ref, wg_ref, gb_ref, z_ref, g_ref, xb_ref, *, tn):
    @pl.when(pl.program_id(1) == 0)
    def _():
        xb_ref[...] = x_ref[...].astype(jnp.bfloat16)

    xb = xb_ref[...]
    for c in range(0, tn, MXU_COLS):
        sl = slice(c, c + MXU_COLS)
        acc = jnp.dot(xb, wg_ref[:, sl], preferred_element_type=jnp.float32)
        g_ref[:, sl] = jax.nn.sigmoid(acc + gb_ref[:, sl]).astype(g_ref.dtype)
    for c in range(0, tn, MXU_COLS):
        sl = slice(c, c + MXU_COLS)
        z_ref[:, sl] = jnp.dot(xb, wz_ref[:, sl], preferred_element_type=jnp.float32).astype(z_ref.dtype)


def _in_proj(x, w_in_b, gate_b, tm=1024, tn=1024):
    s = x.shape[0]
    z_cols = F_WIDTH + 3 * NA_WIDTH
    g_cols = IN_COLS - z_cols
    assert z_cols == g_cols
    n_col = z_cols // tn
    return pl.pallas_call(
        functools.partial(_in_proj_kernel, tn=tn),
        grid=(s // tm, n_col),
        in_specs=[
            pl.BlockSpec((tm, D_MODEL), lambda i, j: (i, 0)),
            pl.BlockSpec((D_MODEL, tn), lambda i, j: (0, j)),
            pl.BlockSpec((D_MODEL, tn), lambda i, j: (0, j + n_col)),
            pl.BlockSpec((1, tn), lambda i, j: (0, j)),
        ],
        out_specs=[pl.BlockSpec((tm, tn), lambda i, j: (i, j)),
                   pl.BlockSpec((tm, tn), lambda i, j: (i, j))],
        out_shape=[jax.ShapeDtypeStruct((s, z_cols), jnp.bfloat16),
                   jax.ShapeDtypeStruct((s, g_cols), jnp.bfloat16)],
        scratch_shapes=[pltpu.VMEM((tm, D_MODEL), jnp.bfloat16)],
        compiler_params=_cparams("parallel", "arbitrary"),
        name="in_proj",
    )(x, w_in_b, w_in_b, gate_b)


def _dft_tables(s):
    n2 = DFT_N2
    n1 = s // n2
    k1 = np.arange(n1)
    w1 = np.exp(-2j * np.pi * np.outer(k1, k1) / n1) / math.sqrt(n1)
    fa = np.concatenate([w1.real, w1.imag], axis=0)
    k2 = np.arange(n2)
    w2 = np.exp(-2j * np.pi * np.outer(k2, k2) / n2) / math.sqrt(n2)
    tw = np.exp(-2j * np.pi * np.outer(k1, k2) / s)
    c = np.arange(F_GROUP_CH)
    wc = np.exp(-2j * np.pi * np.outer(c, c) / F_GROUP_CH) / math.sqrt(F_GROUP_CH)
    wch = np.concatenate([wc.real, -wc.imag], axis=0)
    f32 = lambda a: jnp.asarray(a, dtype=jnp.float32)
    return (f32(fa), f32(w2.real), f32(w2.imag), f32(tw.real), f32(tw.imag), f32(wch))


DFT_SLABS = 16


def _dft_pos_a_kernel(fa_ref, x_ref, o_ref, *, n1):
    fa = fa_ref[...].astype(jnp.bfloat16)
    xt = jnp.swapaxes(x_ref[...], 0, 1)
    re, im = [], []
    for j in range(DFT_SLABS):
        p = jnp.dot(fa, xt[j], preferred_element_type=jnp.float32)
        re.append(p[:n1].astype(o_ref.dtype))
        im.append(p[n1:].astype(o_ref.dtype))
    o_ref[0] = jnp.swapaxes(jnp.stack(re, axis=0), 0, 1)
    o_ref[1] = jnp.swapaxes(jnp.stack(im, axis=0), 0, 1)


def _dft_pos_a(z3, fa):
    n1, n2, _ = z3.shape
    return pl.pallas_call(
        functools.partial(_dft_pos_a_kernel, n1=n1),
        grid=(n2 // DFT_SLABS,),
        in_specs=[
            pl.BlockSpec((2 * n1, n1), lambda i: (0, 0)),
            pl.BlockSpec((n1, DFT_SLABS, F_WIDTH), lambda i: (0, i, 0)),
        ],
        out_specs=pl.BlockSpec((2, n1, DFT_SLABS, F_WIDTH), lambda i: (0, 0, i, 0)),
        out_shape=jax.ShapeDtypeStruct((2, n1, n2, F_WIDTH), jnp.bfloat16),
        compiler_params=_cparams("parallel"),
        name="dft_pos_a",
    )(fa, z3)


def _dft_pos_c_kernel(w2r_ref, w2i_ref, twr_ref, twi_ref, wch_ref, a_ref, o_ref):
    w2r = w2r_ref[...]
    w2i = w2i_ref[...]
    wch = wch_ref[...].astype(jnp.bfloat16)
    slabs = []
    for j in range(DFT_SLABS):
        tr = twr_ref[j:j + 1, :]
        ti = twi_ref[j:j + 1, :]
        gr = (w2r * tr - w2i * ti).astype(jnp.bfloat16)
        gi = (w2r * ti + w2i * tr).astype(jnp.bfloat16)
        g4 = jnp.concatenate(
            [jnp.concatenate([gr, -gi], axis=1), jnp.concatenate([gi, gr], axis=1)], axis=0)
        a = jnp.concatenate([a_ref[0, j], a_ref[1, j]], axis=0)
        y = jnp.dot(g4, a, preferred_element_type=jnp.float32).astype(jnp.bfloat16)
        n2 = y.shape[0] // 2
        yr, yi = y[:n2], y[n2:]
        groups = []
        for g in range(F_GROUPS):
            sl = slice(g * F_GROUP_CH, (g + 1) * F_GROUP_CH)
            lhs = jnp.concatenate([yr[:, sl], yi[:, sl]], axis=1)
            groups.append(jnp.dot(lhs, wch, preferred_element_type=jnp.float32).astype(o_ref.dtype))
        slabs.append(jnp.concatenate(groups, axis=1))
    o_ref[...] = jnp.swapaxes(jnp.stack(slabs, axis=0), 0, 1)


def _dft_pos_c(a, w2r, w2i, twr, twi, wch):
    _, n1, n2, c = a.shape
    full = lambda shape: pl.BlockSpec(shape, lambda i: (0,) * len(shape))
    return pl.pallas_call(
        _dft_pos_c_kernel,
        grid=(n1 // DFT_SLABS,),
        in_specs=[
            full((n2, n2)), full((n2, n2)),
            pl.BlockSpec((DFT_SLABS, n2), lambda i: (i, 0)),
            pl.BlockSpec((DFT_SLABS, n2), lambda i: (i, 0)),
            full((2 * F_GROUP_CH, F_GROUP_CH)),
            pl.BlockSpec((2, DFT_SLABS, n2, c), lambda i: (0, i, 0, 0)),
        ],
        out_specs=pl.BlockSpec((n2, DFT_SLABS, c), lambda i: (0, i, 0)),
        out_shape=jax.ShapeDtypeStruct((n2, n1, c), jnp.bfloat16),
        compiler_params=_cparams("parallel"),
        name="dft_pos_c",
    )(w2r, w2i, twr, twi, wch, a)


def _fourier_mix(z, tables):
    s = z.shape[0]
    n2 = DFT_N2
    n1 = s // n2
    fa, w2r, w2i, twr, twi, wch = tables
    a = _dft_pos_a(z.reshape(n1, n2, z.shape[1]), fa)
    y = _dft_pos_c(a, w2r, w2i, twr, twi, wch)
    return y.reshape(s, F_WIDTH)


NA_RPB_H = 2 * WIN_H - 1
NA_RPB_W = 2 * WIN_W - 1
NA_KROW_PAIRS = NA_KROWS // 2


def _natten_row_windows():
    i = np.arange(NA_QROWS)
    mid = np.full(NA_QROWS, -(WIN_H // 2))
    first = np.where(i < WIN_H // 2, -i, mid)
    last = np.where(i > WIN_H // 2, -i, mid)
    return (first, mid, last)


def _natten_bias_kernel(rpb_ref, o_ref, cb_ref):
    h = pl.program_id(0)
    shape = (GRID_W, 2 * GRID_W)
    qc = lax.broadcasted_iota(jnp.int32, shape, 0)
    lane = lax.broadcasted_iota(jnp.int32, shape, 1)
    kc = lane & (GRID_W - 1)
    col_start = jnp.clip(qc - WIN_W // 2, 0, GRID_W - WIN_W)
    col_ok = (kc >= col_start) & (kc < col_start + WIN_W)
    dc = jnp.where(col_ok, kc - qc + WIN_W - 1, -1)
    neg = jnp.full(shape, NEG_BIAS, jnp.float32)
    for d in range(NA_RPB_H):
        tile = neg
        for t in range(NA_RPB_W):
            r = rpb_ref[(h * NA_RPB_H + d) * NA_RPB_W + t] * (NA_HEAD_DIM ** 0.5)
            tile = jnp.where(dc == t, r, tile)
        cb_ref[d] = tile
    left = lane < GRID_W
    for v, lo in enumerate(_natten_row_windows()):
        for i in range(NA_QROWS):
            for jj in range(NA_KROW_PAIRS):
                dr = (2 * jj - NA_KROW_OFFSET - i, 2 * jj + 1 - NA_KROW_OFFSET - i)
                ok = [lo[i] <= d <= lo[i] + WIN_H - 1 for d in dr]
                halves = [cb_ref[d + WIN_H - 1] if good else neg for d, good in zip(dr, ok)]
                tile = jnp.where(left, halves[0], halves[1]) if any(ok) else neg
                o_ref[v, 0, i * GRID_W:(i + 1) * GRID_W, jj * 2 * GRID_W:(jj + 1) * 2 * GRID_W] = tile


def _natten_bias(rpb):
    blk = NA_QROWS * GRID_W
    keys = NA_KROWS * GRID_W
    return pl.pallas_call(
        _natten_bias_kernel,
        grid=(NA_HEADS,),
        in_specs=[pl.BlockSpec(memory_space=pltpu.SMEM)],
        out_specs=pl.BlockSpec((3, 1, blk, keys), lambda h: (0, h, 0, 0)),
        out_shape=jax.ShapeDtypeStruct((3, NA_HEADS, blk, keys), jnp.float32),
        scratch_shapes=[pltpu.VMEM((NA_RPB_H, GRID_W, 2 * GRID_W), jnp.float32)],
        compiler_params=_cparams("parallel"),
        name="natten_bias",
    )(rpb.astype(jnp.float32).reshape(-1))


NA_HEADS_PER_STEP = 4


def _natten_kernel(q_ref, kp_ref, kc_ref, kn_ref, vp_ref, vc_ref, vn_ref, b_ref, o_ref):
    half = (NA_KROWS - NA_QROWS) // 2 * GRID_W
    blk = NA_QROWS * GRID_W
    exp2_scale = (NA_HEAD_DIM ** -0.5) * math.log2(math.e)
    for h in range(NA_HEADS_PER_STEP):
        sl = slice(h * NA_HEAD_DIM, (h + 1) * NA_HEAD_DIM)
        windows = (
            (slice(0, blk // 2), slice(0, blk + half), (kp_ref, kc_ref), (vp_ref, vc_ref),
             (slice(blk - half, blk), slice(None))),
            (slice(blk // 2, blk), slice(half, blk + 2 * half), (kc_ref, kn_ref), (vc_ref, vn_ref),
             (slice(None), slice(0, half))),
        )
        for rows, keys, k_refs, v_refs, parts in windows:
            kwin = jnp.concatenate([r[p, sl] for r, p in zip(k_refs, parts)], axis=0)
            vwin = jnp.concatenate([r[p, sl] for r, p in zip(v_refs, parts)], axis=0)
            t = lax.dot_general(q_ref[rows, sl], kwin, (((1,), (1,)), ((), ())),
                                preferred_element_type=jnp.float32) + b_ref[0, h, rows, keys]
            m = jnp.max(t, axis=-1, keepdims=True)
            p = jnp.exp2((t - m) * exp2_scale)
            l = jnp.sum(p, axis=-1, keepdims=True)
            o = jnp.dot(p.astype(jnp.bfloat16), vwin, preferred_element_type=jnp.float32)
            o_ref[rows, sl] = (o / l).astype(o_ref.dtype)


def _natten(z, bias):
    s = z.shape[0]
    blk = NA_QROWS * GRID_W
    nb = s // blk
    width = NA_HEADS_PER_STEP * NA_HEAD_DIM
    q0 = F_WIDTH // width
    k0 = q0 + NA_WIDTH // width
    v0 = k0 + NA_WIDTH // width
    prev = lambda b: jnp.maximum(b - 1, 0)
    nxt = lambda b: jnp.minimum(b + 1, nb - 1)
    variant = lambda b: jnp.where(b == 0, 0, jnp.where(b == nb - 1, 2, 1))
    spec = lambda col0, rowfn: pl.BlockSpec((blk, width), lambda h, b: (rowfn(b), col0 + h))
    same = lambda b: b
    return pl.pallas_call(
        _natten_kernel,
        grid=(NA_HEADS // NA_HEADS_PER_STEP, nb),
        in_specs=[
            spec(q0, same),
            spec(k0, prev), spec(k0, same), spec(k0, nxt),
            spec(v0, prev), spec(v0, same), spec(v0, nxt),
            pl.BlockSpec((1, NA_HEADS_PER_STEP, blk, NA_KROWS * GRID_W), lambda h, b: (variant(b), h, 0, 0)),
        ],
        out_specs=pl.BlockSpec((blk, width), lambda h, b: (b, h)),
        out_shape=jax.ShapeDtypeStruct((s, NA_WIDTH), jnp.bfloat16),
        compiler_params=_cparams("parallel", "arbitrary"),
        name="natten",
    )(z, z, z, z, z, z, z, bias)


def _layer_norm(h, g, b):
    mu = jnp.mean(h, axis=-1, keepdims=True)
    d = h - mu
    var = jnp.mean(d * d, axis=-1, keepdims=True)
    return d * lax.rsqrt(var + LN_EPS) * g + b


def _mix_ln1_kernel(x_ref, fm_ref, at_ref, ga_ref, gb_ref, p_ref, fw_ref, nw_ref, wo_ref, pg_ref, pp_ref,
                    g_ref, b_ref, r_ref, x1b_ref, h_ref):
    @pl.when(pl.program_id(0) == 0)
    def _():
        h_ref[...] = jnp.zeros_like(h_ref)

    x1 = _layer_norm(h_ref[...], g_ref[...], b_ref[...])
    x1b = x1.astype(jnp.bfloat16)
    gate = jax.nn.sigmoid(jnp.dot(x1b, pg_ref[...], preferred_element_type=jnp.float32))
    emb = jnp.dot(p_ref[...].astype(jnp.bfloat16), pp_ref[...], preferred_element_type=jnp.float32)
    r_ref[...] = DN_ALPHA * x1 + gate * emb
    x1b_ref[...] = x1b

    a = jnp.dot(fm_ref[...], fw_ref[...], preferred_element_type=jnp.float32)
    b = jnp.dot(at_ref[...], nw_ref[...], preferred_element_type=jnp.float32)
    merged = ga_ref[...].astype(jnp.float32) * a + gb_ref[...].astype(jnp.float32) * b
    h_ref[...] = DN_ALPHA * x_ref[...] + jnp.dot(merged.astype(jnp.bfloat16), wo_ref[...],
                                                 preferred_element_type=jnp.float32)


def _resident(shape):
    return pl.BlockSpec(shape, lambda *_: (0,) * len(shape), pipeline_mode=pl.Buffered(1))


def _mix_ln1(x, fmix, att, gates, p, fw_b, nw_b, wo_b, pg_b, pp_b, ln_g, ln_b, tm=256):
    s = x.shape[0]
    n = s // tm
    cur = lambda width, col=0: pl.BlockSpec((tm, width), lambda i: (jnp.minimum(i, n - 1), col))
    lag = lambda width: pl.BlockSpec((tm, width), lambda i: (jnp.maximum(i - 1, 0), 0))
    return pl.pallas_call(
        _mix_ln1_kernel,
        grid=(n + 1,),
        in_specs=[
            cur(D_MODEL), cur(F_WIDTH), cur(NA_WIDTH), cur(D_MODEL, 0), cur(D_MODEL, 1), lag(PLE_DIM),
            _resident((F_WIDTH, D_MODEL)), _resident((NA_WIDTH, D_MODEL)), _resident((D_MODEL, D_MODEL)),
            _resident((D_MODEL, D_MODEL)), _resident((PLE_DIM, D_MODEL)),
            _resident((1, D_MODEL)), _resident((1, D_MODEL)),
        ],
        out_specs=[lag(D_MODEL), lag(D_MODEL)],
        out_shape=[jax.ShapeDtypeStruct((s, D_MODEL), jnp.float32),
                   jax.ShapeDtypeStruct((s, D_MODEL), jnp.bfloat16)],
        scratch_shapes=[pltpu.VMEM((tm, D_MODEL), jnp.float32)],
        compiler_params=_cparams("arbitrary"),
        name="mix_ln1",
    )(x, fmix, att, gates, gates, p, fw_b, nw_b, wo_b, pg_b, pp_b, ln_g, ln_b)


FFN_HALO = 16


def _gelu_tanh(u):
    return 0.5 * u * (1.0 + jnp.tanh(math.sqrt(2.0 / math.pi) * (u + 0.044715 * (u * u * u))))


LANES = 128
FFN_ROW_BLOCK = 128


def _ffn_up_kernel(xp_ref, xc_ref, xn_ref, wa_ref, wb_ref, ca_ref, cb_ref, ba_ref, bb_ref, h_ref,
                   xe_ref, ua_ref, ub_ref, *, tm, tn, n_row_tiles):
    i = pl.program_id(1)
    xe_ref[:FFN_HALO, :] = jnp.where(i > 0, xp_ref[...], jnp.zeros_like(xp_ref))
    xe_ref[FFN_HALO:FFN_HALO + tm, :] = xc_ref[...]
    xe_ref[FFN_HALO + tm:, :] = jnp.where(i < n_row_tiles - 1, xn_ref[...], jnp.zeros_like(xn_ref))
    xe = xe_ref[...]
    lanes_per_chunk = MXU_COLS // LANES

    def matmul_chunk(c, slot):
        sl = slice(c * MXU_COLS, (c + 1) * MXU_COLS)
        for w_ref, u_ref in ((wa_ref, ua_ref), (wb_ref, ub_ref)):
            u = jnp.dot(xe, w_ref[:, sl], preferred_element_type=jnp.float32)
            for k in range(lanes_per_chunk):
                u_ref[slot, k] = u[:, k * LANES:(k + 1) * LANES]

    def epilogue_chunk(c, slot):
        for k in range(lanes_per_chunk):
            sl = slice(c * MXU_COLS + k * LANES, c * MXU_COLS + (k + 1) * LANES)
            taps = []
            for c_ref, b_ref in ((ca_ref, ba_ref), (cb_ref, bb_ref)):
                taps.append([jnp.broadcast_to(c_ref[t:t + 1, sl], (FFN_ROW_BLOCK, LANES)) for t in range(3)]
                            + [jnp.broadcast_to(b_ref[:, sl], (FFN_ROW_BLOCK, LANES))])
            for r in range(0, tm, FFN_ROW_BLOCK):
                halves = []
                for u_ref, (c0, c1, c2, bias) in zip((ua_ref, ub_ref), taps):
                    lo = u_ref[slot, k, pl.ds(FFN_HALO - 1 + r, FFN_ROW_BLOCK), :]
                    mid = u_ref[slot, k, pl.ds(FFN_HALO + r, FFN_ROW_BLOCK), :]
                    hi = u_ref[slot, k, pl.ds(FFN_HALO + 1 + r, FFN_ROW_BLOCK), :]
                    halves.append(lo * c0 + mid * c1 + hi * c2 + bias)
                h_ref[pl.ds(r, FFN_ROW_BLOCK), sl] = (_gelu_tanh(halves[0]) * halves[1]).astype(h_ref.dtype)

    n_chunks = tn // MXU_COLS
    for c in range(n_chunks):
        matmul_chunk(c, c % 2)
        if c > 0:
            epilogue_chunk(c - 1, (c - 1) % 2)
    epilogue_chunk(n_chunks - 1, (n_chunks - 1) % 2)


def _ffn_up(x1b, up_b, conv, conv_b, tm=1024, tn=D_FF // 2):
    s = x1b.shape[0]
    n_row_tiles = s // tm
    n_col = D_FF // tn
    halo_per_tile = tm // FFN_HALO
    n_halo_blocks = s // FFN_HALO
    prev = lambda j, i: (jnp.maximum(i * halo_per_tile - 1, 0), 0)
    nxt = lambda j, i: (jnp.minimum((i + 1) * halo_per_tile, n_halo_blocks - 1), 0)
    wspec = lambda shape, off: pl.BlockSpec(shape, lambda j, i: (0, j + off), pipeline_mode=pl.Buffered(1))
    return pl.pallas_call(
        functools.partial(_ffn_up_kernel, tm=tm, tn=tn, n_row_tiles=n_row_tiles),
        grid=(n_col, n_row_tiles),
        in_specs=[
            pl.BlockSpec((FFN_HALO, D_MODEL), prev),
            pl.BlockSpec((tm, D_MODEL), lambda j, i: (i, 0)),
            pl.BlockSpec((FFN_HALO, D_MODEL), nxt),
            wspec((D_MODEL, tn), 0), wspec((D_MODEL, tn), n_col),
            wspec((3, tn), 0), wspec((3, tn), n_col),
            wspec((1, tn), 0), wspec((1, tn), n_col),
        ],
        out_specs=pl.BlockSpec((tm, tn), lambda j, i: (i, j)),
        out_shape=jax.ShapeDtypeStruct((s, D_FF), jnp.bfloat16),
        scratch_shapes=[pltpu.VMEM((tm + 2 * FFN_HALO, D_MODEL), jnp.bfloat16),
                        pltpu.VMEM((2, MXU_COLS // LANES, tm + 2 * FFN_HALO, LANES), jnp.float32),
                        pltpu.VMEM((2, MXU_COLS // LANES, tm + 2 * FFN_HALO, LANES), jnp.float32)],
        compiler_params=_cparams("arbitrary", "arbitrary"),
        name="ffn_up",
    )(x1b, x1b, x1b, up_b, up_b, conv, conv, conv_b, conv_b)


def _ffn_down_ln2_kernel(r_ref, h_ref, wd_ref, g_ref, b_ref, o_ref, acc_ref, sum_ref, *, n_row_tiles):
    i = pl.program_id(0)
    k = pl.program_id(1)
    partial = lambda: jnp.dot(h_ref[...], wd_ref[...], preferred_element_type=jnp.float32)
    norm_previous = lambda: _layer_norm(sum_ref[...], g_ref[...], b_ref[...])

    @pl.when((i == 0) & (k == 0))
    def _():
        sum_ref[...] = jnp.zeros_like(sum_ref)

    @pl.when((k == 0) & (i < n_row_tiles))
    def _():
        o_ref[...] = norm_previous()
        acc_ref[...] = r_ref[...] + partial()

    @pl.when((k == 0) & (i == n_row_tiles))
    def _():
        o_ref[...] = norm_previous()

    @pl.when((k == 1) & (i < n_row_tiles))
    def _():
        sum_ref[...] = acc_ref[...] + partial()


def _ffn_down_ln2(r, h, wd_b, ln_g, ln_b, tm=512):
    s = r.shape[0]
    n = s // tm
    tk = D_FF // 2
    assert tk % MXU_COLS == 0
    cur = lambda i: jnp.minimum(i, n - 1)
    return pl.pallas_call(
        functools.partial(_ffn_down_ln2_kernel, n_row_tiles=n),
        grid=(n + 1, 2),
        in_specs=[
            pl.BlockSpec((tm, D_MODEL), lambda i, k: (cur(i), 0)),
            pl.BlockSpec((tm, tk), lambda i, k: (cur(i), k)),
            pl.BlockSpec((tk, D_MODEL), lambda i, k: (k, 0)),
            _resident((1, D_MODEL)), _resident((1, D_MODEL)),
        ],
        out_specs=pl.BlockSpec((tm, D_MODEL), lambda i, k: (jnp.maximum(i - 1, 0), 0)),
        out_shape=jax.ShapeDtypeStruct((s, D_MODEL), jnp.float32),
        scratch_shapes=[pltpu.VMEM((tm, D_MODEL), jnp.float32), pltpu.VMEM((tm, D_MODEL), jnp.float32)],
        compiler_params=_cparams("arbitrary", "arbitrary"),
        name="ffn_down_ln2",
    )(r, h, wd_b, ln_g, ln_b)


def _encoder_layer(x, p, wts, bias, tables):
    z, gates = _in_proj(x, wts["w_in"], wts["gate_b"])
    fmix = _fourier_mix(z, tables)
    att = _natten(z, bias)
    r, x1b = _mix_ln1(x, fmix, att, gates, p, wts["fourier_w"], wts["natten_w"], wts["w_out"],
                      wts["ple_gate"], wts["ple_proj"], wts["ln1_g"], wts["ln1_b"])
    h = _ffn_up(x1b, wts["ffn_up"], wts["ffn_conv"], wts["ffn_conv_b"])
    return _ffn_down_ln2(r, h, wts["ffn_down"], wts["ln2_g"], wts["ln2_b"])


def kernel(x_prompt, x_sample, p_prompt, p_sample, w_in, gate_b, fourier_w, natten_rpb, natten_w, w_out,
           ln1_g, ln1_b, ffn_up, ffn_conv, ffn_conv_b, ffn_down, ple_proj, ple_gate, ln2_g, ln2_b):
    assert w_in.shape[0] == DEPTH
    xp, xs = x_prompt[0], x_sample[0]
    bf = lambda w: w.astype(jnp.bfloat16)
    row = lambda v: v.reshape(1, -1)
    for i in range(DEPTH):
        wts = {
            "w_in": bf(w_in[i]), "gate_b": row(gate_b[i]),
            "fourier_w": bf(fourier_w[i]), "natten_w": bf(natten_w[i]), "w_out": bf(w_out[i]),
            "ln1_g": row(ln1_g[i]), "ln1_b": row(ln1_b[i]),
            "ffn_up": bf(ffn_up[i]), "ffn_conv": ffn_conv[i], "ffn_conv_b": row(ffn_conv_b[i]),
            "ffn_down": bf(ffn_down[i]), "ple_proj": bf(ple_proj[i]), "ple_gate": bf(ple_gate[i]),
            "ln2_g": row(ln2_g[i]), "ln2_b": row(ln2_b[i]),
        }
        bias = _natten_bias(natten_rpb[i])
        xp = _encoder_layer(xp, p_prompt[i, 0], wts, bias, _dft_tables(xp.shape[0]))
        xs = _encoder_layer(xs, p_sample[i, 0], wts, bias, _dft_tables(xs.shape[0]))
    return (xp[None], xs[None])
```

```python
import functools
import math

import jax
import jax.numpy as jnp
import numpy as np
from jax import lax
from jax.experimental import pallas as pl
from jax.experimental.pallas import tpu as pltpu

D_MODEL = 2048
GRID_W = 64
PLE_DIM = 256
F_WIDTH = 1024
F_GROUPS = 8
F_GROUP_CH = F_WIDTH // F_GROUPS
NA_HEADS = 8
NA_HEAD_DIM = 128
NA_WIDTH = NA_HEADS * NA_HEAD_DIM
WIN_H = 8
WIN_W = 16
D_FF = 5632
IN_COLS = F_WIDTH + 3 * NA_WIDTH + 2 * D_MODEL
LN_EPS = 1e-5
DEPTH = 1
DN_ALPHA = (2.0 * DEPTH) ** 0.25

VMEM_LIMIT_BYTES = 56 * 1024 * 1024
NEG_BIAS = -1e30

DFT_N2 = 128
NA_QROWS = 8
NA_KROWS = 16
NA_KROW_OFFSET = 4


def _cparams(*semantics):
    return pltpu.CompilerParams(dimension_semantics=semantics, vmem_limit_bytes=VMEM_LIMIT_BYTES)


MXU_COLS = 256


def _in_proj_kernel(x_ref, wz_ref, wg_ref, gb_ref, z_ref, g_ref, xb_ref, *, tn):
    @pl.when(pl.program_id(1) == 0)
    def _():
        xb_ref[...] = x_ref[...].astype(jnp.bfloat16)

    xb = xb_ref[...]
    for c in range(0, tn, MXU_COLS):
        sl = slice(c, c + MXU_COLS)
        acc = jnp.dot(xb, wg_ref[:, sl], preferred_element_type=jnp.float32)
        g_ref[:, sl] = jax.nn.sigmoid(acc + gb_ref[:, sl]).astype(g_ref.dtype)
    for c in range(0, tn, MXU_COLS):
        sl = slice(c, c + MXU_COLS)
        z_ref[:, sl] = jnp.dot(xb, wz_ref[:, sl], preferred_element_type=jnp.float32).astype(z_ref.dtype)


def _in_proj(x, w_in_b, gate_b, tm=1024, tn=1024):
    s = x.shape[0]
    z_cols = F_WIDTH + 3 * NA_WIDTH
    g_cols = IN_COLS - z_cols
    assert z_cols == g_cols
    n_col = z_cols // tn
    return pl.pallas_call(
        functools.partial(_in_proj_kernel, tn=tn),
        grid=(s // tm, n_col),
        in_specs=[
            pl.BlockSpec((tm, D_MODEL), lambda i, j: (i, 0)),
            pl.BlockSpec((D_MODEL, tn), lambda i, j: (0, j)),
            pl.BlockSpec((D_MODEL, tn), lambda i, j: (0, j + n_col)),
            pl.BlockSpec((1, tn), lambda i, j: (0, j)),
        ],
        out_specs=[pl.BlockSpec((tm, tn), lambda i, j: (i, j)),
                   pl.BlockSpec((tm, tn), lambda i, j: (i, j))],
        out_shape=[jax.ShapeDtypeStruct((s, z_cols), jnp.bfloat16),
                   jax.ShapeDtypeStruct((s, g_cols), jnp.bfloat16)],
        scratch_shapes=[pltpu.VMEM((tm, D_MODEL), jnp.bfloat16)],
        compiler_params=_cparams("parallel", "arbitrary"),
        name="in_proj",
    )(x, w_in_b, w_in_b, gate_b)


def _dft_tables(s):
    n2 = DFT_N2
    n1 = s // n2
    k1 = np.arange(n1)
    w1 = np.exp(-2j * np.pi * np.outer(k1, k1) / n1) / math.sqrt(n1)
    fa = np.concatenate([w1.real, w1.imag], axis=0)
    k2 = np.arange(n2)
    w2 = np.exp(-2j * np.pi * np.outer(k2, k2) / n2) / math.sqrt(n2)
    tw = np.exp(-2j * np.pi * np.outer(k1, k2) / s)
    c = np.arange(F_GROUP_CH)
    wc = np.exp(-2j * np.pi * np.outer(c, c) / F_GROUP_CH) / math.sqrt(F_GROUP_CH)
    wch = np.concatenate([wc.real, -wc.imag], axis=0)
    f32 = lambda a: jnp.asarray(a, dtype=jnp.float32)
    return (f32(fa), f32(w2.real), f32(w2.imag), f32(tw.real), f32(tw.imag), f32(wch))


DFT_SLABS = 16


def _dft_pos_a_kernel(fa_ref, x_ref, o_ref, *, n1):
    fa = fa_ref[...].astype(jnp.bfloat16)
    xt = jnp.swapaxes(x_ref[...], 0, 1)
    re, im = [], []
    for j in range(DFT_SLABS):
        p = jnp.dot(fa, xt[j], preferred_element_type=jnp.float32)
        re.append(p[:n1].astype(o_ref.dtype))
        im.append(p[n1:].astype(o_ref.dtype))
    o_ref[0] = jnp.swapaxes(jnp.stack(re, axis=0), 0, 1)
    o_ref[1] = jnp.swapaxes(jnp.stack(im, axis=0), 0, 1)


def _dft_pos_a(z3, fa):
    n1, n2, _ = z3.shape
    return pl.pallas_call(
        functools.partial(_dft_pos_a_kernel, n1=n1),
        grid=(n2 // DFT_SLABS,),
        in_specs=[
            pl.BlockSpec((2 * n1, n1), lambda i: (0, 0)),
            pl.BlockSpec((n1, DFT_SLABS, F_WIDTH), lambda i: (0, i, 0)),
        ],
        out_specs=pl.BlockSpec((2, n1, DFT_SLABS, F_WIDTH), lambda i: (0, 0, i, 0)),
        out_shape=jax.ShapeDtypeStruct((2, n1, n2, F_WIDTH), jnp.bfloat16),
        compiler_params=_cparams("parallel"),
        name="dft_pos_a",
    )(fa, z3)


def _dft_pos_c_kernel(w2r_ref, w2i_ref, twr_ref, twi_ref, wch_ref, a_ref, o_ref):
    w2r = w2r_ref[...]
    w2i = w2i_ref[...]
    wch = wch_ref[...].astype(jnp.bfloat16)
    slabs = []
    for j in range(DFT_SLABS):
        tr = twr_ref[j:j + 1, :]
        ti = twi_ref[j:j + 1, :]
        gr = (w2r * tr - w2i * ti).astype(jnp.bfloat16)
        gi = (w2r * ti + w2i * tr).astype(jnp.bfloat16)
        g4 = jnp.concatenate(
            [jnp.concatenate([gr, -gi], axis=1), jnp.concatenate([gi, gr], axis=1)], axis=0)
        a = jnp.concatenate([a_ref[0, j], a_ref[1, j]], axis=0)
        y = jnp.dot(g4, a, preferred_element_type=jnp.float32).astype(jnp.bfloat16)
        n2 = y.shape[0] // 2
        yr, yi = y[:n2], y[n2:]
        groups = []
        for g in range(F_GROUPS):
            sl = slice(g * F_GROUP_CH, (g + 1) * F_GROUP_CH)
            lhs = jnp.concatenate([yr[:, sl], yi[:, sl]], axis=1)
            groups.append(jnp.dot(lhs, wch, preferred_element_type=jnp.float32).astype(o_ref.dtype))
        slabs.append(jnp.concatenate(groups, axis=1))
    o_ref[...] = jnp.swapaxes(jnp.stack(slabs, axis=0), 0, 1)


def _dft_pos_c(a, w2r, w2i, twr, twi, wch):
    _, n1, n2, c = a.shape
    full = lambda shape: pl.BlockSpec(shape, lambda i: (0,) * len(shape))
    return pl.pallas_call(
        _dft_pos_c_kernel,
        grid=(n1 // DFT_SLABS,),
        in_specs=[
            full((n2, n2)), full((n2, n2)),
            pl.BlockSpec((DFT_SLABS, n2), lambda i: (i, 0)),
            pl.BlockSpec((DFT_SLABS, n2), lambda i: (i, 0)),
            full((2 * F_GROUP_CH, F_GROUP_CH)),
            pl.BlockSpec((2, DFT_SLABS, n2, c), lambda i: (0, i, 0, 0)),
        ],
        out_specs=pl.BlockSpec((n2, DFT_SLABS, c), lambda i: (0, i, 0)),
        out_shape=jax.ShapeDtypeStruct((n2, n1, c), jnp.bfloat16),
        compiler_params=_cparams("parallel"),
        name="dft_pos_c",
    )(w2r, w2i, twr, twi, wch, a)


def _fourier_mix(z, tables):
    s = z.shape[0]
    n2 = DFT_N2
    n1 = s // n2
    fa, w2r, w2i, twr, twi, wch = tables
    a = _dft_pos_a(z.reshape(n1, n2, z.shape[1]), fa)
    y = _dft_pos_c(a, w2r, w2i, twr, twi, wch)
    return y.reshape(s, F_WIDTH)


NA_RPB_H = 2 * WIN_H - 1
NA_RPB_W = 2 * WIN_W - 1
NA_KROW_PAIRS = NA_KROWS // 2


def _natten_row_windows():
    i = np.arange(NA_QROWS)
    mid = np.full(NA_QROWS, -(WIN_H // 2))
    first = np.where(i < WIN_H // 2, -i, mid)
    last = np.where(i > WIN_H // 2, -i, mid)
    return (first, mid, last)


def _natten_bias_kernel(rpb_ref, o_ref, cb_ref):
    h = pl.program_id(0)
    shape = (GRID_W, 2 * GRID_W)
    qc = lax.broadcasted_iota(jnp.int32, shape, 0)
    lane = lax.broadcasted_iota(jnp.int32, shape, 1)
    kc = lane & (GRID_W - 1)
    col_start = jnp.clip(qc - WIN_W // 2, 0, GRID_W - WIN_W)
    col_ok = (kc >= col_start) & (kc < col_start + WIN_W)
    dc = jnp.where(col_ok, kc - qc + WIN_W - 1, -1)
    neg = jnp.full(shape, NEG_BIAS, jnp.float32)
    for d in range(NA_RPB_H):
        tile = neg
        for t in range(NA_RPB_W):
            r = rpb_ref[(h * NA_RPB_H + d) * NA_RPB_W + t] * (NA_HEAD_DIM ** 0.5)
            tile = jnp.where(dc == t, r, tile)
        cb_ref[d] = tile
    left = lane < GRID_W
    for v, lo in enumerate(_natten_row_windows()):
        for i in range(NA_QROWS):
            for jj in range(NA_KROW_PAIRS):
                dr = (2 * jj - NA_KROW_OFFSET - i, 2 * jj + 1 - NA_KROW_OFFSET - i)
                ok = [lo[i] <= d <= lo[i] + WIN_H - 1 for d in dr]
                halves = [cb_ref[d + WIN_H - 1] if good else neg for d, good in zip(dr, ok)]
                tile = jnp.where(left, halves[0], halves[1]) if any(ok) else neg
                o_ref[v, 0, i * GRID_W:(i + 1) * GRID_W, jj * 2 * GRID_W:(jj + 1) * 2 * GRID_W] = tile


def _natten_bias(rpb):
    blk = NA_QROWS * GRID_W
    keys = NA_KROWS * GRID_W
    return pl.pallas_call(
        _natten_bias_kernel,
        grid=(NA_HEADS,),
        in_specs=[pl.BlockSpec(memory_space=pltpu.SMEM)],
        out_specs=pl.BlockSpec((3, 1, blk, keys), lambda h: (0, h, 0, 0)),
        out_shape=jax.ShapeDtypeStruct((3, NA_HEADS, blk, keys), jnp.float32),
        scratch_shapes=[pltpu.VMEM((NA_RPB_H, GRID_W, 2 * GRID_W), jnp.float32)],
        compiler_params=_cparams("parallel"),
        name="natten_bias",
    )(rpb.astype(jnp.float32).reshape(-1))


NA_HEADS_PER_STEP = 4


NA_ROW_BLOCK = 32


def _natten_kernel(q_ref, kp_ref, kc_ref, kn_ref, vp_ref, vc_ref, vn_ref, b_ref, o_ref, t_ref, p_ref):
    half = (NA_KROWS - NA_QROWS) // 2 * GRID_W
    blk = NA_QROWS * GRID_W
    nq = blk // 2
    nkeys = blk + half
    exp2_scale = (NA_HEAD_DIM ** -0.5) * math.log2(math.e)
    ones = jnp.ones((nkeys, NA_HEAD_DIM), jnp.bfloat16)
    for h in range(NA_HEADS_PER_STEP):
        sl = slice(h * NA_HEAD_DIM, (h + 1) * NA_HEAD_DIM)
        windows = (
            (0, 0, (kp_ref, kc_ref), (vp_ref, vc_ref), (slice(blk - half, blk), slice(None))),
            (nq, half, (kc_ref, kn_ref), (vc_ref, vn_ref), (slice(None), slice(0, half))),
        )
        for w, (row0, key0, k_refs, v_refs, parts) in enumerate(windows):
            ch = 2 * h + w
            kwin = jnp.concatenate([r[p, sl] for r, p in zip(k_refs, parts)], axis=0)
            vwin = jnp.concatenate([r[p, sl] for r, p in zip(v_refs, parts)], axis=0)
            t_ref[ch] = lax.dot_general(q_ref[row0:row0 + nq, sl], kwin, (((1,), (1,)), ((), ())),
                                        preferred_element_type=jnp.float32)
            biased = lambda r0: (t_ref[ch, r0:r0 + NA_ROW_BLOCK, :]
                                 + b_ref[0, h, row0 + r0:row0 + r0 + NA_ROW_BLOCK, key0:key0 + nkeys])
            row_blocks = range(0, nq, NA_ROW_BLOCK)
            maxes = [jnp.max(biased(r0), axis=-1, keepdims=True) for r0 in row_blocks]
            for m, r0 in zip(maxes, row_blocks):
                p_ref[ch, r0:r0 + NA_ROW_BLOCK, :] = jnp.exp2((biased(r0) - m) * exp2_scale).astype(p_ref.dtype)
            ol = jnp.dot(p_ref[ch], jnp.concatenate([vwin, ones], axis=1), preferred_element_type=jnp.float32)
            o_ref[row0:row0 + nq, sl] = (ol[:, :NA_HEAD_DIM] / ol[:, NA_HEAD_DIM:]).astype(o_ref.dtype)


def _natten(z, bias):
    s = z.shape[0]
    blk = NA_QROWS * GRID_W
    nb = s // blk
    width = NA_HEADS_PER_STEP * NA_HEAD_DIM
    q0 = F_WIDTH // width
    k0 = q0 + NA_WIDTH // width
    v0 = k0 + NA_WIDTH // width
    prev = lambda b: jnp.maximum(b - 1, 0)
    nxt = lambda b: jnp.minimum(b + 1, nb - 1)
    variant = lambda b: jnp.where(b == 0, 0, jnp.where(b == nb - 1, 2, 1))
    spec = lambda col0, rowfn: pl.BlockSpec((blk, width), lambda h, b: (rowfn(b), col0 + h))
    same = lambda b: b
    return pl.pallas_call(
        _natten_kernel,
        grid=(NA_HEADS // NA_HEADS_PER_STEP, nb),
        in_specs=[
            spec(q0, same),
            spec(k0, prev), spec(k0, same), spec(k0, nxt),
            spec(v0, prev), spec(v0, same), spec(v0, nxt),
            pl.BlockSpec((1, NA_HEADS_PER_STEP, blk, NA_KROWS * GRID_W), lambda h, b: (variant(b), h, 0, 0)),
        ],
        out_specs=pl.BlockSpec((blk, width), lambda h, b: (b, h)),
        out_shape=jax.ShapeDtypeStruct((s, NA_WIDTH), jnp.bfloat16),
        scratch_shapes=[pltpu.VMEM((2 * NA_HEADS_PER_STEP, blk // 2, blk + blk // 2), jnp.float32),
                        pltpu.VMEM((2 * NA_HEADS_PER_STEP, blk // 2, blk + blk // 2), jnp.bfloat16)],
        compiler_params=_cparams("parallel", "arbitrary"),
        name="natten",
    )(z, z, z, z, z, z, z, bias)


def _layer_norm(h, g, b):
    mu = jnp.mean(h, axis=-1, keepdims=True)
    d = h - mu
    var = jnp.mean(d * d, axis=-1, keepdims=True)
    return d * lax.rsqrt(var + LN_EPS) * g + b


def _mix_ln1_kernel(x_ref, fm_ref, at_ref, ga_ref, gb_ref, p_ref, fw_ref, nw_ref, wo_ref, pg_ref, pp_ref,
                    g_ref, b_ref, r_ref, x1b_ref, h_ref):
    @pl.when(pl.program_id(0) == 0)
    def _():
        h_ref[...] = jnp.zeros_like(h_ref)

    x1 = _layer_norm(h_ref[...], g_ref[...], b_ref[...])
    x1b = x1.astype(jnp.bfloat16)
    gate = jax.nn.sigmoid(jnp.dot(x1b, pg_ref[...], preferred_element_type=jnp.float32))
    emb = jnp.dot(p_ref[...].astype(jnp.bfloat16), pp_ref[...], preferred_element_type=jnp.float32)
    r_ref[...] = DN_ALPHA * x1 + gate * emb
    x1b_ref[...] = x1b

    a = jnp.dot(fm_ref[...], fw_ref[...], preferred_element_type=jnp.float32)
    b = jnp.dot(at_ref[...], nw_ref[...], preferred_element_type=jnp.float32)
    merged = ga_ref[...].astype(jnp.float32) * a + gb_ref[...].astype(jnp.float32) * b
    h_ref[...] = DN_ALPHA * x_ref[...] + jnp.dot(merged.astype(jnp.bfloat16), wo_ref[...],
                                                 preferred_element_type=jnp.float32)


def _resident(shape):
    return pl.BlockSpec(shape, lambda *_: (0,) * len(shape), pipeline_mode=pl.Buffered(1))


def _mix_ln1(x, fmix, att, gates, p, fw_b, nw_b, wo_b, pg_b, pp_b, ln_g, ln_b, tm=256):
    s = x.shape[0]
    n = s // tm
    cur = lambda width, col=0: pl.BlockSpec((tm, width), lambda i: (jnp.minimum(i, n - 1), col))
    lag = lambda width: pl.BlockSpec((tm, width), lambda i: (jnp.maximum(i - 1, 0), 0))
    return pl.pallas_call(
        _mix_ln1_kernel,
        grid=(n + 1,),
        in_specs=[
            cur(D_MODEL), cur(F_WIDTH), cur(NA_WIDTH), cur(D_MODEL, 0), cur(D_MODEL, 1), lag(PLE_DIM),
            _resident((F_WIDTH, D_MODEL)), _resident((NA_WIDTH, D_MODEL)), _resident((D_MODEL, D_MODEL)),
            _resident((D_MODEL, D_MODEL)), _resident((PLE_DIM, D_MODEL)),
            _resident((1, D_MODEL)), _resident((1, D_MODEL)),
        ],
        out_specs=[lag(D_MODEL), lag(D_MODEL)],
        out_shape=[jax.ShapeDtypeStruct((s, D_MODEL), jnp.float32),
                   jax.ShapeDtypeStruct((s, D_MODEL), jnp.bfloat16)],
        scratch_shapes=[pltpu.VMEM((tm, D_MODEL), jnp.float32)],
        compiler_params=_cparams("arbitrary"),
        name="mix_ln1",
    )(x, fmix, att, gates, gates, p, fw_b, nw_b, wo_b, pg_b, pp_b, ln_g, ln_b)


FFN_HALO = 16


def _gelu_tanh(u):
    return 0.5 * u * (1.0 + jnp.tanh(math.sqrt(2.0 / math.pi) * (u + 0.044715 * (u * u * u))))


LANES = 128
FFN_ROW_BLOCK = 128


def _ffn_up_kernel(xp_ref, xc_ref, xn_ref, wa_ref, wb_ref, ca_ref, cb_ref, ba_ref, bb_ref, h_ref,
                   xe_ref, ua_ref, ub_ref, *, tm, tn, n_row_tiles):
    i = pl.program_id(1)
    xe_ref[:FFN_HALO, :] = jnp.where(i > 0, xp_ref[...], jnp.zeros_like(xp_ref))
    xe_ref[FFN_HALO:FFN_HALO + tm, :] = xc_ref[...]
    xe_ref[FFN_HALO + tm:, :] = jnp.where(i < n_row_tiles - 1, xn_ref[...], jnp.zeros_like(xn_ref))
    xe = xe_ref[...]
    lanes_per_chunk = MXU_COLS // LANES

    def matmul_chunk(c, slot):
        sl = slice(c * MXU_COLS, (c + 1) * MXU_COLS)
        for w_ref, u_ref in ((wa_ref, ua_ref), (wb_ref, ub_ref)):
            u = jnp.dot(xe, w_ref[:, sl], preferred_element_type=jnp.float32)
            for k in range(lanes_per_chunk):
                u_ref[slot, k] = u[:, k * LANES:(k + 1) * LANES]

    def epilogue_chunk(c, slot):
        for k in range(lanes_per_chunk):
            sl = slice(c * MXU_COLS + k * LANES, c * MXU_COLS + (k + 1) * LANES)
            taps = []
            for c_ref, b_ref in ((ca_ref, ba_ref), (cb_ref, bb_ref)):
                taps.append([jnp.broadcast_to(c_ref[t:t + 1, sl], (FFN_ROW_BLOCK, LANES)) for t in range(3)]
                            + [jnp.broadcast_to(b_ref[:, sl], (FFN_ROW_BLOCK, LANES))])
            for r in range(0, tm, FFN_ROW_BLOCK):
                halves = []
                for u_ref, (c0, c1, c2, bias) in zip((ua_ref, ub_ref), taps):
                    lo = u_ref[slot, k, pl.ds(FFN_HALO - 1 + r, FFN_ROW_BLOCK), :]
                    mid = u_ref[slot, k, pl.ds(FFN_HALO + r, FFN_ROW_BLOCK), :]
                    hi = u_ref[slot, k, pl.ds(FFN_HALO + 1 + r, FFN_ROW_BLOCK), :]
                    halves.append(lo * c0 + mid * c1 + hi * c2 + bias)
                h_ref[pl.ds(r, FFN_ROW_BLOCK), sl] = (_gelu_tanh(halves[0]) * halves[1]).astype(h_ref.dtype)

    n_chunks = tn // MXU_COLS
    for c in range(n_chunks):
        matmul_chunk(c, c % 2)
        if c > 0:
            epilogue_chunk(c - 1, (c - 1) % 2)
    epilogue_chunk(n_chunks - 1, (n_chunks - 1) % 2)


def _ffn_up(x1b, up_b, conv, conv_b, tm=1024, tn=D_FF // 2):
    s = x1b.shape[0]
    n_row_tiles = s // tm
    n_col = D_FF // tn
    halo_per_tile = tm // FFN_HALO
    n_halo_blocks = s // FFN_HALO
    prev = lambda j, i: (jnp.maximum(i * halo_per_tile - 1, 0), 0)
    nxt = lambda j, i: (jnp.minimum((i + 1) * halo_per_tile, n_halo_blocks - 1), 0)
    wspec = lambda shape, off: pl.BlockSpec(shape, lambda j, i: (0, j + off), pipeline_mode=pl.Buffered(1))
    return pl.pallas_call(
        functools.partial(_ffn_up_kernel, tm=tm, tn=tn, n_row_tiles=n_row_tiles),
        grid=(n_col, n_row_tiles),
        in_specs=[
            pl.BlockSpec((FFN_HALO, D_MODEL), prev),
            pl.BlockSpec((tm, D_MODEL), lambda j, i: (i, 0)),
            pl.BlockSpec((FFN_HALO, D_MODEL), nxt),
            wspec((D_MODEL, tn), 0), wspec((D_MODEL, tn), n_col),
            wspec((3, tn), 0), wspec((3, tn), n_col),
            wspec((1, tn), 0), wspec((1, tn), n_col),
        ],
        out_specs=pl.BlockSpec((tm, tn), lambda j, i: (i, j)),
        out_shape=jax.ShapeDtypeStruct((s, D_FF), jnp.bfloat16),
        scratch_shapes=[pltpu.VMEM((tm + 2 * FFN_HALO, D_MODEL), jnp.bfloat16),
                        pltpu.VMEM((2, MXU_COLS // LANES, tm + 2 * FFN_HALO, LANES), jnp.float32),
                        pltpu.VMEM((2, MXU_COLS // LANES, tm + 2 * FFN_HALO, LANES), jnp.float32)],
        compiler_params=_cparams("arbitrary", "arbitrary"),
        name="ffn_up",
    )(x1b, x1b, x1b, up_b, up_b, conv, conv, conv_b, conv_b)


def _ffn_down_ln2_kernel(r_ref, h_ref, wd_ref, g_ref, b_ref, o_ref, acc_ref, sum_ref, *, n_row_tiles):
    i = pl.program_id(0)
    k = pl.program_id(1)
    partial = lambda: jnp.dot(h_ref[...], wd_ref[...], preferred_element_type=jnp.float32)
    norm_previous = lambda: _layer_norm(sum_ref[...], g_ref[...], b_ref[...])

    @pl.when((i == 0) & (k == 0))
    def _():
        sum_ref[...] = jnp.zeros_like(sum_ref)

    @pl.when((k == 0) & (i < n_row_tiles))
    def _():
        o_ref[...] = norm_previous()
        acc_ref[...] = r_ref[...] + partial()

    @pl.when((k == 0) & (i == n_row_tiles))
    def _():
        o_ref[...] = norm_previous()

    @pl.when((k == 1) & (i < n_row_tiles))
    def _():
        sum_ref[...] = acc_ref[...] + partial()


def _ffn_down_ln2(r, h, wd_b, ln_g, ln_b, tm=512):
    s = r.shape[0]
    n = s // tm
    tk = D_FF // 2
    assert tk % MXU_COLS == 0
    cur = lambda i: jnp.minimum(i, n - 1)
    return pl.pallas_call(
        functools.partial(_ffn_down_ln2_kernel, n_row_tiles=n),
        grid=(n + 1, 2),
        in_specs=[
            pl.BlockSpec((tm, D_MODEL), lambda i, k: (cur(i), 0)),
            pl.BlockSpec((tm, tk), lambda i, k: (cur(i), k)),
            pl.BlockSpec((tk, D_MODEL), lambda i, k: (k, 0)),
            _resident((1, D_MODEL)), _resident((1, D_MODEL)),
        ],
        out_specs=pl.BlockSpec((tm, D_MODEL), lambda i, k: (jnp.maximum(i - 1, 0), 0)),
        out_shape=jax.ShapeDtypeStruct((s, D_MODEL), jnp.float32),
        scratch_shapes=[pltpu.VMEM((tm, D_MODEL), jnp.float32), pltpu.VMEM((tm, D_MODEL), jnp.float32)],
        compiler_params=_cparams("arbitrary", "arbitrary"),
        name="ffn_down_ln2",
    )(r, h, wd_b, ln_g, ln_b)


def _encoder_layer(x, p, wts, bias, tables):
    z, gates = _in_proj(x, wts["w_in"], wts["gate_b"])
    fmix = _fourier_mix(z, tables)
    att = _natten(z, bias)
    r, x1b = _mix_ln1(x, fmix, att, gates, p, wts["fourier_w"], wts["natten_w"], wts["w_out"],
                      wts["ple_gate"], wts["ple_proj"], wts["ln1_g"], wts["ln1_b"])
    h = _ffn_up(x1b, wts["ffn_up"], wts["ffn_conv"], wts["ffn_conv_b"])
    return _ffn_down_ln2(r, h, wts["ffn_down"], wts["ln2_g"], wts["ln2_b"])


def kernel(x_prompt, x_sample, p_prompt, p_sample, w_in, gate_b, fourier_w, natten_rpb, natten_w, w_out,
           ln1_g, ln1_b, ffn_up, ffn_conv, ffn_conv_b, ffn_down, ple_proj, ple_gate, ln2_g, ln2_b):
    assert w_in.shape[0] == DEPTH
    xp, xs = x_prompt[0], x_sample[0]
    bf = lambda w: w.astype(jnp.bfloat16)
    row = lambda v: v.reshape(1, -1)
    for i in range(DEPTH):
        wts = {
            "w_in": bf(w_in[i]), "gate_b": row(gate_b[i]),
            "fourier_w": bf(fourier_w[i]), "natten_w": bf(natten_w[i]), "w_out": bf(w_out[i]),
            "ln1_g": row(ln1_g[i]), "ln1_b": row(ln1_b[i]),
            "ffn_up": bf(ffn_up[i]), "ffn_conv": ffn_conv[i], "ffn_conv_b": row(ffn_conv_b[i]),
            "ffn_down": bf(ffn_down[i]), "ple_proj": bf(ple_proj[i]), "ple_gate": bf(ple_gate[i]),
            "ln2_g": row(ln2_g[i]), "ln2_b": row(ln2_b[i]),
        }
        bias = _natten_bias(natten_rpb[i])
        xp = _encoder_layer(xp, p_prompt[i, 0], wts, bias, _dft_tables(xp.shape[0]))
        xs = _encoder_layer(xs, p_sample[i, 0], wts, bias, _dft_tables(xs.shape[0]))
    return (xp[None], xs[None])
```

```python
import functools
import math

import jax
import jax.numpy as jnp
import numpy as np
from jax import lax
from jax.experimental import pallas as pl
from jax.experimental.pallas import tpu as pltpu

D_MODEL = 2048
GRID_W = 64
PLE_DIM = 256
F_WIDTH = 1024
F_GROUPS = 8
F_GROUP_CH = F_WIDTH // F_GROUPS
NA_HEADS = 8
NA_HEAD_DIM = 128
NA_WIDTH = NA_HEADS * NA_HEAD_DIM
WIN_H = 8
WIN_W = 16
D_FF = 5632
IN_COLS = F_WIDTH + 3 * NA_WIDTH + 2 * D_MODEL
LN_EPS = 1e-5
DEPTH = 1
DN_ALPHA = (2.0 * DEPTH) ** 0.25

VMEM_PHYSICAL_BYTES = 64 * 1024 * 1024
VMEM_LIMIT_BYTES = 56 * 1024 * 1024
VMEM_INTERNAL_SCRATCH_BYTES = 6 * 1024 * 1024
NEG_BIAS = -1e30

DFT_N2 = 128
NA_QROWS = 8
NA_KROWS = 16
NA_KROW_OFFSET = 4


def _cparams(*semantics, vmem_limit_bytes=VMEM_LIMIT_BYTES):
    return pltpu.CompilerParams(dimension_semantics=semantics, vmem_limit_bytes=vmem_limit_bytes)


MXU_COLS = 256


CAST_ROW_ALIGN = 16


def _cast_rows_per_step(rows, n_steps):
    per_step = -(-rows // n_steps)
    per_step = -(-per_step // CAST_ROW_ALIGN) * CAST_ROW_ALIGN
    while rows % per_step:
        per_step += CAST_ROW_ALIGN
    return per_step


def _in_proj_kernel(*refs, tn, n_cast):
    x_ref, wz_ref, wg_ref, gb_ref = refs[:4]
    cast_in = refs[4:4 + n_cast]
    z_ref, g_ref = refs[4 + n_cast:6 + n_cast]
    cast_out = refs[6 + n_cast:6 + 2 * n_cast]
    xb_ref = refs[-1]

    @pl.when(pl.program_id(1) == 0)
    def _():
        xb_ref[...] = x_ref[...].astype(jnp.bfloat16)

    for src_ref, dst_ref in zip(cast_in, cast_out):
        dst_ref[...] = src_ref[...].astype(dst_ref.dtype)

    xb = xb_ref[...]
    for c in range(0, tn, MXU_COLS):
        sl = slice(c, c + MXU_COLS)
        acc = jnp.dot(xb, wg_ref[:, sl], preferred_element_type=jnp.float32)
        g_ref[:, sl] = jax.nn.sigmoid(acc + gb_ref[:, sl]).astype(g_ref.dtype)
    for c in range(0, tn, MXU_COLS):
        sl = slice(c, c + MXU_COLS)
        z_ref[:, sl] = jnp.dot(xb, wz_ref[:, sl], preferred_element_type=jnp.float32).astype(z_ref.dtype)


def _in_proj(x, w_in_b, gate_b, cast_weights=(), tm=1024, tn=1024):
    s = x.shape[0]
    z_cols = F_WIDTH + 3 * NA_WIDTH
    g_cols = IN_COLS - z_cols
    assert z_cols == g_cols
    n_col = z_cols // tn
    n_steps = (s // tm) * n_col
    cast_rows = [_cast_rows_per_step(w.shape[0], n_steps) for w in cast_weights]
    cast_blocks = tuple(w.shape[0] // r for w, r in zip(cast_weights, cast_rows))
    vmem_bytes = (2 * tm * D_MODEL * 4 + 2 * 2 * D_MODEL * tn * 2 + 2 * 2 * tm * tn * 2 + tm * D_MODEL * 2
                  + sum(2 * r * w.shape[1] * (4 + 2) for w, r in zip(cast_weights, cast_rows))
                  + VMEM_INTERNAL_SCRATCH_BYTES)
    assert vmem_bytes <= VMEM_PHYSICAL_BYTES - VMEM_INTERNAL_SCRATCH_BYTES // 2, vmem_bytes
    cast_specs = [
        pl.BlockSpec((r, w.shape[1]), lambda i, j, nb=nb: (jnp.minimum(i * n_col + j, nb - 1), 0))
        for w, r, nb in zip(cast_weights, cast_rows, cast_blocks)]
    outs = pl.pallas_call(
        functools.partial(_in_proj_kernel, tn=tn, n_cast=len(cast_weights)),
        grid=(s // tm, n_col),
        in_specs=[
            pl.BlockSpec((tm, D_MODEL), lambda i, j: (i, 0)),
            pl.BlockSpec((D_MODEL, tn), lambda i, j: (0, j)),
            pl.BlockSpec((D_MODEL, tn), lambda i, j: (0, j + n_col)),
            pl.BlockSpec((1, tn), lambda i, j: (0, j)),
        ] + cast_specs,
        out_specs=[pl.BlockSpec((tm, tn), lambda i, j: (i, j)),
                   pl.BlockSpec((tm, tn), lambda i, j: (i, j))] + cast_specs,
        out_shape=[jax.ShapeDtypeStruct((s, z_cols), jnp.bfloat16),
                   jax.ShapeDtypeStruct((s, g_cols), jnp.bfloat16)]
        + [jax.ShapeDtypeStruct(w.shape, jnp.bfloat16) for w in cast_weights],
        scratch_shapes=[pltpu.VMEM((tm, D_MODEL), jnp.bfloat16)],
        compiler_params=_cparams("arbitrary", "arbitrary", vmem_limit_bytes=vmem_bytes),
        name="in_proj",
    )(x, w_in_b, w_in_b, gate_b, *cast_weights)
    return outs[0], outs[1], tuple(outs[2:])


def _dft_tables(s):
    n2 = DFT_N2
    n1 = s // n2
    k1 = np.arange(n1)
    w1 = np.exp(-2j * np.pi * np.outer(k1, k1) / n1) / math.sqrt(n1)
    fa = np.concatenate([w1.real, w1.imag], axis=0)
    k2 = np.arange(n2)
    w2 = np.exp(-2j * np.pi * np.outer(k2, k2) / n2) / math.sqrt(n2)
    tw = np.exp(-2j * np.pi * np.outer(k1, k2) / s)
    c = np.arange(F_GROUP_CH)
    wc = np.exp(-2j * np.pi * np.outer(c, c) / F_GROUP_CH) / math.sqrt(F_GROUP_CH)
    wch = np.concatenate([wc.real, -wc.imag], axis=0)
    f32 = lambda a: jnp.asarray(a, dtype=jnp.float32)
    return (f32(fa), f32(w2.real), f32(w2.imag), f32(tw.real), f32(tw.imag), f32(wch))


DFT_SLABS = 16


def _dft_pos_a_kernel(fa_ref, x_ref, o_ref, *, n1):
    fa = fa_ref[...].astype(jnp.bfloat16)
    xt = jnp.swapaxes(x_ref[...], 0, 1)
    re, im = [], []
    for j in range(DFT_SLABS):
        p = jnp.dot(fa, xt[j], preferred_element_type=jnp.float32)
        re.append(p[:n1].astype(o_ref.dtype))
        im.append(p[n1:].astype(o_ref.dtype))
    o_ref[0] = jnp.swapaxes(jnp.stack(re, axis=0), 0, 1)
    o_ref[1] = jnp.swapaxes(jnp.stack(im, axis=0), 0, 1)


def _dft_pos_a(z3, fa):
    n1, n2, _ = z3.shape
    return pl.pallas_call(
        functools.partial(_dft_pos_a_kernel, n1=n1),
        grid=(n2 // DFT_SLABS,),
        in_specs=[
            pl.BlockSpec((2 * n1, n1), lambda i: (0, 0)),
            pl.BlockSpec((n1, DFT_SLABS, F_WIDTH), lambda i: (0, i, 0)),
        ],
        out_specs=pl.BlockSpec((2, n1, DFT_SLABS, F_WIDTH), lambda i: (0, 0, i, 0)),
        out_shape=jax.ShapeDtypeStruct((2, n1, n2, F_WIDTH), jnp.bfloat16),
        compiler_params=_cparams("parallel"),
        name="dft_pos_a",
    )(fa, z3)


def _dft_pos_c_kernel(w2r_ref, w2i_ref, twr_ref, twi_ref, wch_ref, a_ref, o_ref):
    w2r = w2r_ref[...]
    w2i = w2i_ref[...]
    wch = wch_ref[...].astype(jnp.bfloat16)
    slabs = []
    for j in range(DFT_SLABS):
        tr = twr_ref[j:j + 1, :]
        ti = twi_ref[j:j + 1, :]
        gr = (w2r * tr - w2i * ti).astype(jnp.bfloat16)
        gi = (w2r * ti + w2i * tr).astype(jnp.bfloat16)
        g4 = jnp.concatenate(
            [jnp.concatenate([gr, -gi], axis=1), jnp.concatenate([gi, gr], axis=1)], axis=0)
        a = jnp.concatenate([a_ref[0, j], a_ref[1, j]], axis=0)
        y = jnp.dot(g4, a, preferred_element_type=jnp.float32).astype(jnp.bfloat16)
        n2 = y.shape[0] // 2
        yr, yi = y[:n2], y[n2:]
        groups = []
        for g in range(F_GROUPS):
            sl = slice(g * F_GROUP_CH, (g + 1) * F_GROUP_CH)
            lhs = jnp.concatenate([yr[:, sl], yi[:, sl]], axis=1)
            groups.append(jnp.dot(lhs, wch, preferred_element_type=jnp.float32).astype(o_ref.dtype))
        slabs.append(jnp.concatenate(groups, axis=1))
    o_ref[...] = jnp.swapaxes(jnp.stack(slabs, axis=0), 0, 1)


def _dft_pos_c(a, w2r, w2i, twr, twi, wch):
    _, n1, n2, c = a.shape
    full = lambda shape: pl.BlockSpec(shape, lambda i: (0,) * len(shape))
    return pl.pallas_call(
        _dft_pos_c_kernel,
        grid=(n1 // DFT_SLABS,),
        in_specs=[
            full((n2, n2)), full((n2, n2)),
            pl.BlockSpec((DFT_SLABS, n2), lambda i: (i, 0)),
            pl.BlockSpec((DFT_SLABS, n2), lambda i: (i, 0)),
            full((2 * F_GROUP_CH, F_GROUP_CH)),
            pl.BlockSpec((2, DFT_SLABS, n2, c), lambda i: (0, i, 0, 0)),
        ],
        out_specs=pl.BlockSpec((n2, DFT_SLABS, c), lambda i: (0, i, 0)),
        out_shape=jax.ShapeDtypeStruct((n2, n1, c), jnp.bfloat16),
        compiler_params=_cparams("parallel"),
        name="dft_pos_c",
    )(w2r, w2i, twr, twi, wch, a)


def _fourier_mix(z, tables):
    s = z.shape[0]
    n2 = DFT_N2
    n1 = s // n2
    fa, w2r, w2i, twr, twi, wch = tables
    a = _dft_pos_a(z.reshape(n1, n2, z.shape[1]), fa)
    y = _dft_pos_c(a, w2r, w2i, twr, twi, wch)
    return y.reshape(s, F_WIDTH)


NA_RPB_H = 2 * WIN_H - 1
NA_RPB_W = 2 * WIN_W - 1
NA_KROW_PAIRS = NA_KROWS // 2


def _natten_row_windows():
    i = np.arange(NA_QROWS)
    mid = np.full(NA_QROWS, -(WIN_H // 2))
    first = np.where(i < WIN_H // 2, -i, mid)
    last = np.where(i > WIN_H // 2, -i, mid)
    return (first, mid, last)


def _natten_bias_kernel(rpb_ref, o_ref, cb_ref):
    h = pl.program_id(0)
    shape = (GRID_W, 2 * GRID_W)
    qc = lax.broadcasted_iota(jnp.int32, shape, 0)
    lane = lax.broadcasted_iota(jnp.int32, shape, 1)
    kc = lane & (GRID_W - 1)
    col_start = jnp.clip(qc - WIN_W // 2, 0, GRID_W - WIN_W)
    col_ok = (kc >= col_start) & (kc < col_start + WIN_W)
    dc = jnp.where(col_ok, kc - qc + WIN_W - 1, -1)
    neg = jnp.full(shape, NEG_BIAS, jnp.float32)
    for d in range(NA_RPB_H):
        tile = neg
        for t in range(NA_RPB_W):
            r = rpb_ref[(h * NA_RPB_H + d) * NA_RPB_W + t] * (NA_HEAD_DIM ** 0.5)
            tile = jnp.where(dc == t, r, tile)
        cb_ref[d] = tile
    left = lane < GRID_W
    for v, lo in enumerate(_natten_row_windows()):
        for i in range(NA_QROWS):
            for jj in range(NA_KROW_PAIRS):
                dr = (2 * jj - NA_KROW_OFFSET - i, 2 * jj + 1 - NA_KROW_OFFSET - i)
                ok = [lo[i] <= d <= lo[i] + WIN_H - 1 for d in dr]
                halves = [cb_ref[d + WIN_H - 1] if good else neg for d, good in zip(dr, ok)]
                tile = jnp.where(left, halves[0], halves[1]) if any(ok) else neg
                o_ref[v, 0, i * GRID_W:(i + 1) * GRID_W, jj * 2 * GRID_W:(jj + 1) * 2 * GRID_W] = tile


def _natten_bias(rpb):
    blk = NA_QROWS * GRID_W
    keys = NA_KROWS * GRID_W
    return pl.pallas_call(
        _natten_bias_kernel,
        grid=(NA_HEADS,),
        in_specs=[pl.BlockSpec(memory_space=pltpu.SMEM)],
        out_specs=pl.BlockSpec((3, 1, blk, keys), lambda h: (0, h, 0, 0)),
        out_shape=jax.ShapeDtypeStruct((3, NA_HEADS, blk, keys), jnp.float32),
        scratch_shapes=[pltpu.VMEM((NA_RPB_H, GRID_W, 2 * GRID_W), jnp.float32)],
        compiler_params=_cparams("parallel"),
        name="natten_bias",
    )(rpb.astype(jnp.float32).reshape(-1))


NA_HEADS_PER_STEP = 4


NA_ROW_BLOCK = 32


def _natten_kernel(q_ref, kp_ref, kc_ref, kn_ref, vp_ref, vc_ref, vn_ref, b_ref, o_ref, t_ref, p_ref):
    half = (NA_KROWS - NA_QROWS) // 2 * GRID_W
    blk = NA_QROWS * GRID_W
    nq = blk // 2
    nkeys = blk + half
    exp2_scale = (NA_HEAD_DIM ** -0.5) * math.log2(math.e)
    ones = jnp.ones((nkeys, NA_HEAD_DIM), jnp.bfloat16)
    for h in range(NA_HEADS_PER_STEP):
        sl = slice(h * NA_HEAD_DIM, (h + 1) * NA_HEAD_DIM)
        windows = (
            (0, 0, (kp_ref, kc_ref), (vp_ref, vc_ref), (slice(blk - half, blk), slice(None))),
            (nq, half, (kc_ref, kn_ref), (vc_ref, vn_ref), (slice(None), slice(0, half))),
        )
        for w, (row0, key0, k_refs, v_refs, parts) in enumerate(windows):
            ch = 2 * h + w
            kwin = jnp.concatenate([r[p, sl] for r, p in zip(k_refs, parts)], axis=0)
            vwin = jnp.concatenate([r[p, sl] for r, p in zip(v_refs, parts)], axis=0)
            t_ref[ch] = lax.dot_general(q_ref[row0:row0 + nq, sl], kwin, (((1,), (1,)), ((), ())),
                                        preferred_element_type=jnp.float32)
            biased = lambda r0: (t_ref[ch, r0:r0 + NA_ROW_BLOCK, :]
                                 + b_ref[0, h, row0 + r0:row0 + r0 + NA_ROW_BLOCK, key0:key0 + nkeys])
            row_blocks = range(0, nq, NA_ROW_BLOCK)
            maxes = [jnp.max(biased(r0), axis=-1, keepdims=True) for r0 in row_blocks]
            for m, r0 in zip(maxes, row_blocks):
                p_ref[ch, r0:r0 + NA_ROW_BLOCK, :] = jnp.exp2((biased(r0) - m) * exp2_scale).astype(p_ref.dtype)
            ol = jnp.dot(p_ref[ch], jnp.concatenate([vwin, ones], axis=1), preferred_element_type=jnp.float32)
            o_ref[row0:row0 + nq, sl] = (ol[:, :NA_HEAD_DIM] / ol[:, NA_HEAD_DIM:]).astype(o_ref.dtype)


def _natten(z, bias):
    s = z.shape[0]
    blk = NA_QROWS * GRID_W
    nb = s // blk
    width = NA_HEADS_PER_STEP * NA_HEAD_DIM
    q0 = F_WIDTH // width
    k0 = q0 + NA_WIDTH // width
    v0 = k0 + NA_WIDTH // width
    prev = lambda b: jnp.maximum(b - 1, 0)
    nxt = lambda b: jnp.minimum(b + 1, nb - 1)
    variant = lambda b: jnp.where(b == 0, 0, jnp.where(b == nb - 1, 2, 1))
    spec = lambda col0, rowfn: pl.BlockSpec((blk, width), lambda h, b: (rowfn(b), col0 + h))
    same = lambda b: b
    return pl.pallas_call(
        _natten_kernel,
        grid=(NA_HEADS // NA_HEADS_PER_STEP, nb),
        in_specs=[
            spec(q0, same),
            spec(k0, prev), spec(k0, same), spec(k0, nxt),
            spec(v0, prev), spec(v0, same), spec(v0, nxt),
            pl.BlockSpec((1, NA_HEADS_PER_STEP, blk, NA_KROWS * GRID_W), lambda h, b: (variant(b), h, 0, 0)),
        ],
        out_specs=pl.BlockSpec((blk, width), lambda h, b: (b, h)),
        out_shape=jax.ShapeDtypeStruct((s, NA_WIDTH), jnp.bfloat16),
        scratch_shapes=[pltpu.VMEM((2 * NA_HEADS_PER_STEP, blk // 2, blk + blk // 2), jnp.float32),
                        pltpu.VMEM((2 * NA_HEADS_PER_STEP, blk // 2, blk + blk // 2), jnp.bfloat16)],
        compiler_params=_cparams("parallel", "arbitrary"),
        name="natten",
    )(z, z, z, z, z, z, z, bias)


def _layer_norm(h, g, b):
    mu = jnp.mean(h, axis=-1, keepdims=True)
    d = h - mu
    var = jnp.mean(d * d, axis=-1, keepdims=True)
    return d * lax.rsqrt(var + LN_EPS) * g + b


def _mix_ln1_kernel(x_ref, fm_ref, at_ref, ga_ref, gb_ref, p_ref, fw_ref, nw_ref, wo_ref, pg_ref, pp_ref,
                    g_ref, b_ref, r_ref, x1b_ref, h_ref):
    @pl.when(pl.program_id(0) == 0)
    def _():
        h_ref[...] = jnp.zeros_like(h_ref)

    x1 = _layer_norm(h_ref[...], g_ref[...], b_ref[...])
    x1b = x1.astype(jnp.bfloat16)
    gate = jax.nn.sigmoid(jnp.dot(x1b, pg_ref[...], preferred_element_type=jnp.float32))
    emb = jnp.dot(p_ref[...].astype(jnp.bfloat16), pp_ref[...], preferred_element_type=jnp.float32)
    r_ref[...] = DN_ALPHA * x1 + gate * emb
    x1b_ref[...] = x1b

    a = jnp.dot(fm_ref[...], fw_ref[...], preferred_element_type=jnp.float32)
    b = jnp.dot(at_ref[...], nw_ref[...], preferred_element_type=jnp.float32)
    merged = ga_ref[...].astype(jnp.float32) * a + gb_ref[...].astype(jnp.float32) * b
    h_ref[...] = DN_ALPHA * x_ref[...] + jnp.dot(merged.astype(jnp.bfloat16), wo_ref[...],
                                                 preferred_element_type=jnp.float32)


def _resident(shape):
    return pl.BlockSpec(shape, lambda *_: (0,) * len(shape), pipeline_mode=pl.Buffered(1))


def _mix_ln1(x, fmix, att, gates, p, fw_b, nw_b, wo_b, pg_b, pp_b, ln_g, ln_b, tm=256):
    s = x.shape[0]
    n = s // tm
    cur = lambda width, col=0: pl.BlockSpec((tm, width), lambda i: (jnp.minimum(i, n - 1), col))
    lag = lambda width: pl.BlockSpec((tm, width), lambda i: (jnp.maximum(i - 1, 0), 0))
    return pl.pallas_call(
        _mix_ln1_kernel,
        grid=(n + 1,),
        in_specs=[
            cur(D_MODEL), cur(F_WIDTH), cur(NA_WIDTH), cur(D_MODEL, 0), cur(D_MODEL, 1), lag(PLE_DIM),
            _resident((F_WIDTH, D_MODEL)), _resident((NA_WIDTH, D_MODEL)), _resident((D_MODEL, D_MODEL)),
            _resident((D_MODEL, D_MODEL)), _resident((PLE_DIM, D_MODEL)),
            _resident((1, D_MODEL)), _resident((1, D_MODEL)),
        ],
        out_specs=[lag(D_MODEL), lag(D_MODEL)],
        out_shape=[jax.ShapeDtypeStruct((s, D_MODEL), jnp.float32),
                   jax.ShapeDtypeStruct((s, D_MODEL), jnp.bfloat16)],
        scratch_shapes=[pltpu.VMEM((tm, D_MODEL), jnp.float32)],
        compiler_params=_cparams("arbitrary"),
        name="mix_ln1",
    )(x, fmix, att, gates, gates, p, fw_b, nw_b, wo_b, pg_b, pp_b, ln_g, ln_b)


FFN_HALO = 16


def _gelu_tanh(u):
    return 0.5 * u * (1.0 + jnp.tanh(math.sqrt(2.0 / math.pi) * (u + 0.044715 * (u * u * u))))


LANES = 128
FFN_ROW_BLOCK = 128


def _ffn_up_kernel(xp_ref, xc_ref, xn_ref, wa_ref, wb_ref, ca_ref, cb_ref, ba_ref, bb_ref, h_ref,
                   xe_ref, ua_ref, ub_ref, *, tm, tn, n_row_tiles):
    i = pl.program_id(1)
    xe_ref[:FFN_HALO, :] = jnp.where(i > 0, xp_ref[...], jnp.zeros_like(xp_ref))
    xe_ref[FFN_HALO:FFN_HALO + tm, :] = xc_ref[...]
    xe_ref[FFN_HALO + tm:, :] = jnp.where(i < n_row_tiles - 1, xn_ref[...], jnp.zeros_like(xn_ref))
    xe = xe_ref[...]
    lanes_per_chunk = MXU_COLS // LANES

    def matmul_chunk(c, slot):
        sl = slice(c * MXU_COLS, (c + 1) * MXU_COLS)
        for w_ref, u_ref in ((wa_ref, ua_ref), (wb_ref, ub_ref)):
            u = jnp.dot(xe, w_ref[:, sl], preferred_element_type=jnp.float32)
            for k in range(lanes_per_chunk):
                u_ref[slot, k] = u[:, k * LANES:(k + 1) * LANES]

    def epilogue_chunk(c, slot):
        for k in range(lanes_per_chunk):
            sl = slice(c * MXU_COLS + k * LANES, c * MXU_COLS + (k + 1) * LANES)
            taps = []
            for c_ref, b_ref in ((ca_ref, ba_ref), (cb_ref, bb_ref)):
                taps.append([jnp.broadcast_to(c_ref[t:t + 1, sl], (FFN_ROW_BLOCK, LANES)) for t in range(3)]
                            + [jnp.broadcast_to(b_ref[:, sl], (FFN_ROW_BLOCK, LANES))])
            for r in range(0, tm, FFN_ROW_BLOCK):
                halves = []
                for u_ref, (c0, c1, c2, bias) in zip((ua_ref, ub_ref), taps):
                    lo = u_ref[slot, k, pl.ds(FFN_HALO - 1 + r, FFN_ROW_BLOCK), :]
                    mid = u_ref[slot, k, pl.ds(FFN_HALO + r, FFN_ROW_BLOCK), :]
                    hi = u_ref[slot, k, pl.ds(FFN_HALO + 1 + r, FFN_ROW_BLOCK), :]
                    halves.append(lo * c0 + mid * c1 + hi * c2 + bias)
                h_ref[pl.ds(r, FFN_ROW_BLOCK), sl] = (_gelu_tanh(halves[0]) * halves[1]).astype(h_ref.dtype)

    n_chunks = tn // MXU_COLS
    for c in range(n_chunks):
        matmul_chunk(c, c % 2)
        if c > 0:
            epilogue_chunk(c - 1, (c - 1) % 2)
    epilogue_chunk(n_chunks - 1, (n_chunks - 1) % 2)


def _ffn_up(x1b, up_b, conv, conv_b, tm=1024, tn=D_FF // 2):
    s = x1b.shape[0]
    n_row_tiles = s // tm
    n_col = D_FF // tn
    halo_per_tile = tm // FFN_HALO
    n_halo_blocks = s // FFN_HALO
    prev = lambda j, i: (jnp.maximum(i * halo_per_tile - 1, 0), 0)
    nxt = lambda j, i: (jnp.minimum((i + 1) * halo_per_tile, n_halo_blocks - 1), 0)
    wspec = lambda shape, off: pl.BlockSpec(shape, lambda j, i: (0, j + off), pipeline_mode=pl.Buffered(1))
    return pl.pallas_call(
        functools.partial(_ffn_up_kernel, tm=tm, tn=tn, n_row_tiles=n_row_tiles),
        grid=(n_col, n_row_tiles),
        in_specs=[
            pl.BlockSpec((FFN_HALO, D_MODEL), prev),
            pl.BlockSpec((tm, D_MODEL), lambda j, i: (i, 0)),
            pl.BlockSpec((FFN_HALO, D_MODEL), nxt),
            wspec((D_MODEL, tn), 0), wspec((D_MODEL, tn), n_col),
            wspec((3, tn), 0), wspec((3, tn), n_col),
            wspec((1, tn), 0), wspec((1, tn), n_col),
        ],
        out_specs=pl.BlockSpec((tm, tn), lambda j, i: (i, j)),
        out_shape=jax.ShapeDtypeStruct((s, D_FF), jnp.bfloat16),
        scratch_shapes=[pltpu.VMEM((tm + 2 * FFN_HALO, D_MODEL), jnp.bfloat16),
                        pltpu.VMEM((2, MXU_COLS // LANES, tm + 2 * FFN_HALO, LANES), jnp.float32),
                        pltpu.VMEM((2, MXU_COLS // LANES, tm + 2 * FFN_HALO, LANES), jnp.float32)],
        compiler_params=_cparams("arbitrary", "arbitrary"),
        name="ffn_up",
    )(x1b, x1b, x1b, up_b, up_b, conv, conv, conv_b, conv_b)


SUBLANES = 8


def _zero_after(x):
    rows, cols = x.shape
    bits = pltpu.bitcast(x, jnp.uint32)
    acc = bits[:SUBLANES]
    for r in range(SUBLANES, rows, SUBLANES):
        acc = acc | bits[r:r + SUBLANES]
    tile = acc[:, :LANES]
    for c in range(LANES, cols, LANES):
        tile = tile | acc[:, c:c + LANES]
    zero = pltpu.bitcast((tile >> 16) >> 16, jnp.float32)
    return pltpu.repeat(pltpu.repeat(zero, rows // SUBLANES, axis=0), cols // LANES, axis=1)


def _ffn_down_ln2_kernel(r_ref, h_ref, wd_ref, g_ref, b_ref, o_ref, acc_ref, sum_ref, *, n_row_tiles):
    i = pl.program_id(0)
    k = pl.program_id(1)
    partial = lambda: jnp.dot(h_ref[...], wd_ref[...], preferred_element_type=jnp.float32)
    norm_previous = lambda: _layer_norm(sum_ref[...], g_ref[...], b_ref[...])

    @pl.when((i == 0) & (k == 0))
    def _():
        sum_ref[...] = jnp.zeros_like(sum_ref)

    @pl.when((k == 0) & (i < n_row_tiles))
    def _():
        y = norm_previous()
        o_ref[...] = y
        acc_ref[...] = (r_ref[...] + _zero_after(y)) + partial()

    @pl.when((k == 0) & (i == n_row_tiles))
    def _():
        o_ref[...] = norm_previous()

    @pl.when((k == 1) & (i < n_row_tiles))
    def _():
        sum_ref[...] = acc_ref[...] + partial()


def _ffn_down_ln2(r, h, wd_b, ln_g, ln_b, tm=512):
    s = r.shape[0]
    n = s // tm
    tk = D_FF // 2
    assert tk % MXU_COLS == 0
    cur = lambda i: jnp.minimum(i, n - 1)
    return pl.pallas_call(
        functools.partial(_ffn_down_ln2_kernel, n_row_tiles=n),
        grid=(n + 1, 2),
        in_specs=[
            pl.BlockSpec((tm, D_MODEL), lambda i, k: (cur(i), 0)),
            pl.BlockSpec((tm, tk), lambda i, k: (cur(i), k)),
            pl.BlockSpec((tk, D_MODEL), lambda i, k: (k, 0)),
            _resident((1, D_MODEL)), _resident((1, D_MODEL)),
        ],
        out_specs=pl.BlockSpec((tm, D_MODEL), lambda i, k: (jnp.maximum(i - 1, 0), 0)),
        out_shape=jax.ShapeDtypeStruct((s, D_MODEL), jnp.float32),
        scratch_shapes=[pltpu.VMEM((tm, D_MODEL), jnp.float32), pltpu.VMEM((tm, D_MODEL), jnp.float32)],
        compiler_params=_cparams("arbitrary", "arbitrary"),
        name="ffn_down_ln2",
    )(r, h, wd_b, ln_g, ln_b)


CAST_IN_KERNEL = ("fourier_w", "natten_w", "w_out", "ple_gate", "ple_proj", "ffn_up", "ffn_down")


def _encoder_layer(x, p, wts, bias, tables):
    pending = [name for name in CAST_IN_KERNEL if wts[name].dtype != jnp.bfloat16]
    z, gates, cast = _in_proj(x, wts["w_in"], wts["gate_b"], [wts[name] for name in pending])
    wts = {**wts, **dict(zip(pending, cast))}
    return _layer_after_in_proj(x, z, gates, p, wts, bias, tables), wts


def _layer_after_in_proj(x, z, gates, p, wts, bias, tables):
    fmix = _fourier_mix(z, tables)
    att = _natten(z, bias)
    r, x1b = _mix_ln1(x, fmix, att, gates, p, wts["fourier_w"], wts["natten_w"], wts["w_out"],
                      wts["ple_gate"], wts["ple_proj"], wts["ln1_g"], wts["ln1_b"])
    h = _ffn_up(x1b, wts["ffn_up"], wts["ffn_conv"], wts["ffn_conv_b"])
    return _ffn_down_ln2(r, h, wts["ffn_down"], wts["ln2_g"], wts["ln2_b"])


def kernel(x_prompt, x_sample, p_prompt, p_sample, w_in, gate_b, fourier_w, natten_rpb, natten_w, w_out,
           ln1_g, ln1_b, ffn_up, ffn_conv, ffn_conv_b, ffn_down, ple_proj, ple_gate, ln2_g, ln2_b):
    assert w_in.shape[0] == DEPTH
    xp, xs = x_prompt[0], x_sample[0]
    bf = lambda w: w.astype(jnp.bfloat16)
    row = lambda v: v.reshape(1, -1)
    for i in range(DEPTH):
        wts = {
            "w_in": bf(w_in[i]), "gate_b": row(gate_b[i]),
            "fourier_w": fourier_w[i], "natten_w": natten_w[i], "w_out": w_out[i],
            "ln1_g": row(ln1_g[i]), "ln1_b": row(ln1_b[i]),
            "ffn_up": ffn_up[i], "ffn_conv": ffn_conv[i], "ffn_conv_b": row(ffn_conv_b[i]),
            "ffn_down": ffn_down[i], "ple_proj": ple_proj[i], "ple_gate": ple_gate[i],
            "ln2_g": row(ln2_g[i]), "ln2_b": row(ln2_b[i]),
        }
        bias = _natten_bias(natten_rpb[i])
        xp, wts = _encoder_layer(xp, p_prompt[i, 0], wts, bias, _dft_tables(xp.shape[0]))
        xs, wts = _encoder_layer(xs, p_sample[i, 0], wts, bias, _dft_tables(xs.shape[0]))
    return (xp[None], xs[None])
```

```python
import functools
import math

import jax
import jax.numpy as jnp
import numpy as np
from jax import lax
from jax.experimental import pallas as pl
from jax.experimental.pallas import tpu as pltpu

D_MODEL = 2048
GRID_W = 64
PLE_DIM = 256
F_WIDTH = 1024
F_GROUPS = 8
F_GROUP_CH = F_WIDTH // F_GROUPS
NA_HEADS = 8
NA_HEAD_DIM = 128
NA_WIDTH = NA_HEADS * NA_HEAD_DIM
WIN_H = 8
WIN_W = 16
D_FF = 5632
IN_COLS = F_WIDTH + 3 * NA_WIDTH + 2 * D_MODEL
LN_EPS = 1e-5
DEPTH = 1
DN_ALPHA = (2.0 * DEPTH) ** 0.25

VMEM_PHYSICAL_BYTES = 64 * 1024 * 1024
VMEM_LIMIT_BYTES = 56 * 1024 * 1024
VMEM_INTERNAL_SCRATCH_BYTES = 6 * 1024 * 1024
NEG_BIAS = -1e30

DFT_N2 = 128
NA_QROWS = 8
NA_KROWS = 16
NA_KROW_OFFSET = 4


def _cparams(*semantics, vmem_limit_bytes=VMEM_LIMIT_BYTES):
    return pltpu.CompilerParams(dimension_semantics=semantics, vmem_limit_bytes=vmem_limit_bytes)


MXU_COLS = 256


CAST_ROW_ALIGN = 16


def _cast_rows_per_step(rows, n_steps):
    per_step = -(-rows // n_steps)
    per_step = -(-per_step // CAST_ROW_ALIGN) * CAST_ROW_ALIGN
    while rows % per_step:
        per_step += CAST_ROW_ALIGN
    return per_step


def _in_proj_kernel(*refs, tn, n_cast):
    x_ref, wz_ref, wg_ref, gb_ref = refs[:4]
    cast_in = refs[4:4 + n_cast]
    z_ref, g_ref = refs[4 + n_cast:6 + n_cast]
    cast_out = refs[6 + n_cast:6 + 2 * n_cast]
    xb_ref = refs[-1]

    @pl.when(pl.program_id(1) == 0)
    def _():
        xb_ref[...] = x_ref[...].astype(jnp.bfloat16)

    for src_ref, dst_ref in zip(cast_in, cast_out):
        dst_ref[...] = src_ref[...].astype(dst_ref.dtype)

    xb = xb_ref[...]
    for c in range(0, tn, MXU_COLS):
        sl = slice(c, c + MXU_COLS)
        acc = jnp.dot(xb, wg_ref[:, sl], preferred_element_type=jnp.float32)
        g_ref[:, sl] = jax.nn.sigmoid(acc + gb_ref[:, sl]).astype(g_ref.dtype)
    for c in range(0, tn, MXU_COLS):
        sl = slice(c, c + MXU_COLS)
        z_ref[:, sl] = jnp.dot(xb, wz_ref[:, sl], preferred_element_type=jnp.float32).astype(z_ref.dtype)


def _in_proj(x, w_in_b, gate_b, cast_weights=(), tm=1024, tn=1024):
    s = x.shape[0]
    z_cols = F_WIDTH + 3 * NA_WIDTH
    g_cols = IN_COLS - z_cols
    assert z_cols == g_cols
    n_col = z_cols // tn
    n_steps = (s // tm) * n_col
    cast_rows = [_cast_rows_per_step(w.shape[0], n_steps) for w in cast_weights]
    cast_blocks = tuple(w.shape[0] // r for w, r in zip(cast_weights, cast_rows))
    vmem_bytes = (2 * tm * D_MODEL * 4 + 2 * 2 * D_MODEL * tn * 2 + 2 * 2 * tm * tn * 2 + tm * D_MODEL * 2
                  + sum(2 * r * w.shape[1] * (4 + 2) for w, r in zip(cast_weights, cast_rows))
                  + VMEM_INTERNAL_SCRATCH_BYTES)
    assert vmem_bytes <= VMEM_PHYSICAL_BYTES - VMEM_INTERNAL_SCRATCH_BYTES // 2, vmem_bytes
    cast_specs = [
        pl.BlockSpec((r, w.shape[1]), lambda i, j, nb=nb: (jnp.minimum(i * n_col + j, nb - 1), 0))
        for w, r, nb in zip(cast_weights, cast_rows, cast_blocks)]
    outs = pl.pallas_call(
        functools.partial(_in_proj_kernel, tn=tn, n_cast=len(cast_weights)),
        grid=(s // tm, n_col),
        in_specs=[
            pl.BlockSpec((tm, D_MODEL), lambda i, j: (i, 0)),
            pl.BlockSpec((D_MODEL, tn), lambda i, j: (0, j)),
            pl.BlockSpec((D_MODEL, tn), lambda i, j: (0, j + n_col)),
            pl.BlockSpec((1, tn), lambda i, j: (0, j)),
        ] + cast_specs,
        out_specs=[pl.BlockSpec((tm, tn), lambda i, j: (i, j)),
                   pl.BlockSpec((tm, tn), lambda i, j: (i, j))] + cast_specs,
        out_shape=[jax.ShapeDtypeStruct((s, z_cols), jnp.bfloat16),
                   jax.ShapeDtypeStruct((s, g_cols), jnp.bfloat16)]
        + [jax.ShapeDtypeStruct(w.shape, jnp.bfloat16) for w in cast_weights],
        scratch_shapes=[pltpu.VMEM((tm, D_MODEL), jnp.bfloat16)],
        compiler_params=_cparams("arbitrary", "arbitrary", vmem_limit_bytes=vmem_bytes),
        name="in_proj",
    )(x, w_in_b, w_in_b, gate_b, *cast_weights)
    return outs[0], outs[1], tuple(outs[2:])


def _dft_tables(s):
    n2 = DFT_N2
    n1 = s // n2
    k1 = np.arange(n1)
    w1 = np.exp(-2j * np.pi * np.outer(k1, k1) / n1) / math.sqrt(n1)
    fa = np.concatenate([w1.real, w1.imag], axis=0)
    k2 = np.arange(n2)
    w2 = np.exp(-2j * np.pi * np.outer(k2, k2) / n2) / math.sqrt(n2)
    tw = np.exp(-2j * np.pi * np.outer(k1, k2) / s)
    c = np.arange(F_GROUP_CH)
    wc = np.exp(-2j * np.pi * np.outer(c, c) / F_GROUP_CH) / math.sqrt(F_GROUP_CH)
    wch = np.concatenate([wc.real, -wc.imag], axis=0)
    f32 = lambda a: jnp.asarray(a, dtype=jnp.float32)
    return (f32(fa), f32(w2.real), f32(w2.imag), f32(tw.real), f32(tw.imag), f32(wch))


DFT_SLABS = 16


def _dft_pos_a_kernel(fa_ref, x_ref, o_ref, *, n1):
    fa = fa_ref[...].astype(jnp.bfloat16)
    xt = jnp.swapaxes(x_ref[...], 0, 1)
    re, im = [], []
    for j in range(DFT_SLABS):
        p = jnp.dot(fa, xt[j], preferred_element_type=jnp.float32)
        re.append(p[:n1].astype(o_ref.dtype))
        im.append(p[n1:].astype(o_ref.dtype))
    o_ref[0] = jnp.swapaxes(jnp.stack(re, axis=0), 0, 1)
    o_ref[1] = jnp.swapaxes(jnp.stack(im, axis=0), 0, 1)


def _dft_pos_a(z3, fa):
    n1, n2, _ = z3.shape
    return pl.pallas_call(
        functools.partial(_dft_pos_a_kernel, n1=n1),
        grid=(n2 // DFT_SLABS,),
        in_specs=[
            pl.BlockSpec((2 * n1, n1), lambda i: (0, 0)),
            pl.BlockSpec((n1, DFT_SLABS, F_WIDTH), lambda i: (0, i, 0)),
        ],
        out_specs=pl.BlockSpec((2, n1, DFT_SLABS, F_WIDTH), lambda i: (0, 0, i, 0)),
        out_shape=jax.ShapeDtypeStruct((2, n1, n2, F_WIDTH), jnp.bfloat16),
        compiler_params=_cparams("parallel"),
        name="dft_pos_a",
    )(fa, z3)


def _dft_pos_c_kernel(w2r_ref, w2i_ref, twr_ref, twi_ref, wch_ref, a_ref, o_ref):
    w2r = w2r_ref[...]
    w2i = w2i_ref[...]
    wch = wch_ref[...].astype(jnp.bfloat16)
    slabs = []
    for j in range(DFT_SLABS):
        tr = twr_ref[j:j + 1, :]
        ti = twi_ref[j:j + 1, :]
        gr = (w2r * tr - w2i * ti).astype(jnp.bfloat16)
        gi = (w2r * ti + w2i * tr).astype(jnp.bfloat16)
        g4 = jnp.concatenate(
            [jnp.concatenate([gr, -gi], axis=1), jnp.concatenate([gi, gr], axis=1)], axis=0)
        a = jnp.concatenate([a_ref[0, j], a_ref[1, j]], axis=0)
        y = jnp.dot(g4, a, preferred_element_type=jnp.float32).astype(jnp.bfloat16)
        n2 = y.shape[0] // 2
        yr, yi = y[:n2], y[n2:]
        groups = []
        for g in range(F_GROUPS):
            sl = slice(g * F_GROUP_CH, (g + 1) * F_GROUP_CH)
            lhs = jnp.concatenate([yr[:, sl], yi[:, sl]], axis=1)
            groups.append(jnp.dot(lhs, wch, preferred_element_type=jnp.float32).astype(o_ref.dtype))
        slabs.append(jnp.concatenate(groups, axis=1))
    o_ref[...] = jnp.swapaxes(jnp.stack(slabs, axis=0), 0, 1)


def _dft_pos_c(a, w2r, w2i, twr, twi, wch):
    _, n1, n2, c = a.shape
    full = lambda shape: pl.BlockSpec(shape, lambda i: (0,) * len(shape))
    return pl.pallas_call(
        _dft_pos_c_kernel,
        grid=(n1 // DFT_SLABS,),
        in_specs=[
            full((n2, n2)), full((n2, n2)),
            pl.BlockSpec((DFT_SLABS, n2), lambda i: (i, 0)),
            pl.BlockSpec((DFT_SLABS, n2), lambda i: (i, 0)),
            full((2 * F_GROUP_CH, F_GROUP_CH)),
            pl.BlockSpec((2, DFT_SLABS, n2, c), lambda i: (0, i, 0, 0)),
        ],
        out_specs=pl.BlockSpec((n2, DFT_SLABS, c), lambda i: (0, i, 0)),
        out_shape=jax.ShapeDtypeStruct((n2, n1, c), jnp.bfloat16),
        compiler_params=_cparams("parallel"),
        name="dft_pos_c",
    )(w2r, w2i, twr, twi, wch, a)


def _fourier_mix(z, tables):
    s = z.shape[0]
    n2 = DFT_N2
    n1 = s // n2
    fa, w2r, w2i, twr, twi, wch = tables
    a = _dft_pos_a(z.reshape(n1, n2, z.shape[1]), fa)
    y = _dft_pos_c(a, w2r, w2i, twr, twi, wch)
    return y.reshape(s, F_WIDTH)


NA_RPB_H = 2 * WIN_H - 1
NA_RPB_W = 2 * WIN_W - 1
NA_KROW_PAIRS = NA_KROWS // 2


def _natten_row_windows():
    i = np.arange(NA_QROWS)
    mid = np.full(NA_QROWS, -(WIN_H // 2))
    first = np.where(i < WIN_H // 2, -i, mid)
    last = np.where(i > WIN_H // 2, -i, mid)
    return (first, mid, last)


def _natten_bias_kernel(rpb_ref, o_ref, cb_ref):
    h = pl.program_id(0)
    shape = (GRID_W, 2 * GRID_W)
    qc = lax.broadcasted_iota(jnp.int32, shape, 0)
    lane = lax.broadcasted_iota(jnp.int32, shape, 1)
    kc = lane & (GRID_W - 1)
    col_start = jnp.clip(qc - WIN_W // 2, 0, GRID_W - WIN_W)
    col_ok = (kc >= col_start) & (kc < col_start + WIN_W)
    dc = jnp.where(col_ok, kc - qc + WIN_W - 1, -1)
    neg = jnp.full(shape, NEG_BIAS, jnp.float32)
    for d in range(NA_RPB_H):
        tile = neg
        for t in range(NA_RPB_W):
            r = rpb_ref[(h * NA_RPB_H + d) * NA_RPB_W + t] * (NA_HEAD_DIM ** 0.5)
            tile = jnp.where(dc == t, r, tile)
        cb_ref[d] = tile
    left = lane < GRID_W
    for v, lo in enumerate(_natten_row_windows()):
        for i in range(NA_QROWS):
            for jj in range(NA_KROW_PAIRS):
                dr = (2 * jj - NA_KROW_OFFSET - i, 2 * jj + 1 - NA_KROW_OFFSET - i)
                ok = [lo[i] <= d <= lo[i] + WIN_H - 1 for d in dr]
                halves = [cb_ref[d + WIN_H - 1] if good else neg for d, good in zip(dr, ok)]
                tile = jnp.where(left, halves[0], halves[1]) if any(ok) else neg
                o_ref[v, 0, i * GRID_W:(i + 1) * GRID_W, jj * 2 * GRID_W:(jj + 1) * 2 * GRID_W] = tile


def _natten_bias(rpb):
    blk = NA_QROWS * GRID_W
    keys = NA_KROWS * GRID_W
    return pl.pallas_call(
        _natten_bias_kernel,
        grid=(NA_HEADS,),
        in_specs=[pl.BlockSpec(memory_space=pltpu.SMEM)],
        out_specs=pl.BlockSpec((3, 1, blk, keys), lambda h: (0, h, 0, 0)),
        out_shape=jax.ShapeDtypeStruct((3, NA_HEADS, blk, keys), jnp.float32),
        scratch_shapes=[pltpu.VMEM((NA_RPB_H, GRID_W, 2 * GRID_W), jnp.float32)],
        compiler_params=_cparams("parallel"),
        name="natten_bias",
    )(rpb.astype(jnp.float32).reshape(-1))


NA_HEADS_PER_STEP = 4


NA_ROW_BLOCK = 32


def _natten_kernel(q_ref, kp_ref, kc_ref, kn_ref, vp_ref, vc_ref, vn_ref, b_ref, o_ref, t_ref, p_ref):
    half = (NA_KROWS - NA_QROWS) // 2 * GRID_W
    blk = NA_QROWS * GRID_W
    nq = blk // 2
    nkeys = blk + half
    exp2_scale = (NA_HEAD_DIM ** -0.5) * math.log2(math.e)
    ones = jnp.ones((nkeys, NA_HEAD_DIM), jnp.bfloat16)
    for h in range(NA_HEADS_PER_STEP):
        sl = slice(h * NA_HEAD_DIM, (h + 1) * NA_HEAD_DIM)
        windows = (
            (0, 0, (kp_ref, kc_ref), (vp_ref, vc_ref), (slice(blk - half, blk), slice(None))),
            (nq, half, (kc_ref, kn_ref), (vc_ref, vn_ref), (slice(None), slice(0, half))),
        )
        for w, (row0, key0, k_refs, v_refs, parts) in enumerate(windows):
            ch = 2 * h + w
            kwin = jnp.concatenate([r[p, sl] for r, p in zip(k_refs, parts)], axis=0)
            vwin = jnp.concatenate([r[p, sl] for r, p in zip(v_refs, parts)], axis=0)
            t_ref[ch] = lax.dot_general(q_ref[row0:row0 + nq, sl], kwin, (((1,), (1,)), ((), ())),
                                        preferred_element_type=jnp.float32)
            biased = lambda r0: (t_ref[ch, r0:r0 + NA_ROW_BLOCK, :]
                                 + b_ref[0, h, row0 + r0:row0 + r0 + NA_ROW_BLOCK, key0:key0 + nkeys])
            row_blocks = range(0, nq, NA_ROW_BLOCK)
            maxes = [jnp.max(biased(r0), axis=-1, keepdims=True) for r0 in row_blocks]
            for m, r0 in zip(maxes, row_blocks):
                p_ref[ch, r0:r0 + NA_ROW_BLOCK, :] = jnp.exp2((biased(r0) - m) * exp2_scale).astype(p_ref.dtype)
            ol = jnp.dot(p_ref[ch], jnp.concatenate([vwin, ones], axis=1), preferred_element_type=jnp.float32)
            o_ref[row0:row0 + nq, sl] = (ol[:, :NA_HEAD_DIM] / ol[:, NA_HEAD_DIM:]).astype(o_ref.dtype)


def _natten(z, bias):
    s = z.shape[0]
    blk = NA_QROWS * GRID_W
    nb = s // blk
    width = NA_HEADS_PER_STEP * NA_HEAD_DIM
    q0 = F_WIDTH // width
    k0 = q0 + NA_WIDTH // width
    v0 = k0 + NA_WIDTH // width
    prev = lambda b: jnp.maximum(b - 1, 0)
    nxt = lambda b: jnp.minimum(b + 1, nb - 1)
    variant = lambda b: jnp.where(b == 0, 0, jnp.where(b == nb - 1, 2, 1))
    spec = lambda col0, rowfn: pl.BlockSpec((blk, width), lambda h, b: (rowfn(b), col0 + h))
    same = lambda b: b
    return pl.pallas_call(
        _natten_kernel,
        grid=(NA_HEADS // NA_HEADS_PER_STEP, nb),
        in_specs=[
            spec(q0, same),
            spec(k0, prev), spec(k0, same), spec(k0, nxt),
            spec(v0, prev), spec(v0, same), spec(v0, nxt),
            pl.BlockSpec((1, NA_HEADS_PER_STEP, blk, NA_KROWS * GRID_W), lambda h, b: (variant(b), h, 0, 0)),
        ],
        out_specs=pl.BlockSpec((blk, width), lambda h, b: (b, h)),
        out_shape=jax.ShapeDtypeStruct((s, NA_WIDTH), jnp.bfloat16),
        scratch_shapes=[pltpu.VMEM((2 * NA_HEADS_PER_STEP, blk // 2, blk + blk // 2), jnp.float32),
                        pltpu.VMEM((2 * NA_HEADS_PER_STEP, blk // 2, blk + blk // 2), jnp.bfloat16)],
        compiler_params=_cparams("parallel", "arbitrary"),
        name="natten",
    )(z, z, z, z, z, z, z, bias)


def _layer_norm(h, g, b):
    mu = jnp.mean(h, axis=-1, keepdims=True)
    d = h - mu
    var = jnp.mean(d * d, axis=-1, keepdims=True)
    return d * lax.rsqrt(var + LN_EPS) * g + b


def _mix_ln1_kernel(x_ref, fm_ref, at_ref, ga_ref, gb_ref, p_ref, fw_ref, nw_ref, wo_ref, pg_ref, pp_ref,
                    g_ref, b_ref, r_ref, x1b_ref, h_ref):
    @pl.when(pl.program_id(0) == 0)
    def _():
        h_ref[...] = jnp.zeros_like(h_ref)

    x1 = _layer_norm(h_ref[...], g_ref[...], b_ref[...])
    x1b = x1.astype(jnp.bfloat16)
    gate = jax.nn.sigmoid(jnp.dot(x1b, pg_ref[...], preferred_element_type=jnp.float32))
    emb = jnp.dot(p_ref[...].astype(jnp.bfloat16), pp_ref[...], preferred_element_type=jnp.float32)
    r_ref[...] = DN_ALPHA * x1 + gate * emb
    x1b_ref[...] = x1b

    a = jnp.dot(fm_ref[...], fw_ref[...], preferred_element_type=jnp.float32)
    b = jnp.dot(at_ref[...], nw_ref[...], preferred_element_type=jnp.float32)
    merged = ga_ref[...].astype(jnp.float32) * a + gb_ref[...].astype(jnp.float32) * b
    h_ref[...] = DN_ALPHA * x_ref[...] + jnp.dot(merged.astype(jnp.bfloat16), wo_ref[...],
                                                 preferred_element_type=jnp.float32)


def _resident(shape):
    return pl.BlockSpec(shape, lambda *_: (0,) * len(shape), pipeline_mode=pl.Buffered(1))


def _mix_ln1(x, fmix, att, gates, p, fw_b, nw_b, wo_b, pg_b, pp_b, ln_g, ln_b, tm=256):
    s = x.shape[0]
    n = s // tm
    cur = lambda width, col=0: pl.BlockSpec((tm, width), lambda i: (jnp.minimum(i, n - 1), col))
    lag = lambda width: pl.BlockSpec((tm, width), lambda i: (jnp.maximum(i - 1, 0), 0))
    return pl.pallas_call(
        _mix_ln1_kernel,
        grid=(n + 1,),
        in_specs=[
            cur(D_MODEL), cur(F_WIDTH), cur(NA_WIDTH), cur(D_MODEL, 0), cur(D_MODEL, 1), lag(PLE_DIM),
            _resident((F_WIDTH, D_MODEL)), _resident((NA_WIDTH, D_MODEL)), _resident((D_MODEL, D_MODEL)),
            _resident((D_MODEL, D_MODEL)), _resident((PLE_DIM, D_MODEL)),
            _resident((1, D_MODEL)), _resident((1, D_MODEL)),
        ],
        out_specs=[lag(D_MODEL), lag(D_MODEL)],
        out_shape=[jax.ShapeDtypeStruct((s, D_MODEL), jnp.float32),
                   jax.ShapeDtypeStruct((s, D_MODEL), jnp.bfloat16)],
        scratch_shapes=[pltpu.VMEM((tm, D_MODEL), jnp.float32)],
        compiler_params=_cparams("arbitrary"),
        name="mix_ln1",
    )(x, fmix, att, gates, gates, p, fw_b, nw_b, wo_b, pg_b, pp_b, ln_g, ln_b)


FFN_HALO = 16


def _gelu_tanh(u):
    return 0.5 * u * (1.0 + jnp.tanh(math.sqrt(2.0 / math.pi) * (u + 0.044715 * (u * u * u))))


LANES = 128
FFN_ROW_BLOCK = 128
FFN_STAGE_SLOTS = 3


def _ffn_up_kernel(xp_ref, xc_ref, xn_ref, wa_ref, wb_ref, ca_ref, cb_ref, ba_ref, bb_ref, h_ref,
                   xe_ref, ua_ref, ub_ref, *, tm, tn, n_row_tiles):
    i = pl.program_id(1)
    xe_ref[:FFN_HALO, :] = jnp.where(i > 0, xp_ref[...], jnp.zeros_like(xp_ref))
    xe_ref[FFN_HALO:FFN_HALO + tm, :] = xc_ref[...]
    xe_ref[FFN_HALO + tm:, :] = jnp.where(i < n_row_tiles - 1, xn_ref[...], jnp.zeros_like(xn_ref))
    lanes_per_chunk = MXU_COLS // LANES

    def matmul_chunk(c, slot):
        sl = slice(c * MXU_COLS, (c + 1) * MXU_COLS)
        for w_ref, u_ref in ((wa_ref, ua_ref), (wb_ref, ub_ref)):
            u = jnp.dot(xe_ref[...], w_ref[:, sl], preferred_element_type=jnp.float32)
            for k in range(lanes_per_chunk):
                u_ref[slot, k] = u[:, k * LANES:(k + 1) * LANES]

    def epilogue_chunk(c, slot):
        for k in range(lanes_per_chunk):
            sl = slice(c * MXU_COLS + k * LANES, c * MXU_COLS + (k + 1) * LANES)
            taps = []
            for c_ref, b_ref in ((ca_ref, ba_ref), (cb_ref, bb_ref)):
                taps.append([jnp.broadcast_to(c_ref[t:t + 1, sl], (FFN_ROW_BLOCK, LANES)) for t in range(3)]
                            + [jnp.broadcast_to(b_ref[:, sl], (FFN_ROW_BLOCK, LANES))])
            for r in range(0, tm, FFN_ROW_BLOCK):
                halves = []
                for u_ref, (c0, c1, c2, bias) in zip((ua_ref, ub_ref), taps):
                    lo = u_ref[slot, k, pl.ds(FFN_HALO - 1 + r, FFN_ROW_BLOCK), :]
                    mid = u_ref[slot, k, pl.ds(FFN_HALO + r, FFN_ROW_BLOCK), :]
                    hi = u_ref[slot, k, pl.ds(FFN_HALO + 1 + r, FFN_ROW_BLOCK), :]
                    halves.append(lo * c0 + mid * c1 + hi * c2 + bias)
                h_ref[pl.ds(r, FFN_ROW_BLOCK), sl] = (_gelu_tanh(halves[0]) * halves[1]).astype(h_ref.dtype)

    n_chunks = tn // MXU_COLS
    for c in range(n_chunks):
        matmul_chunk(c, c % FFN_STAGE_SLOTS)
        if c > 0:
            epilogue_chunk(c - 1, (c - 1) % FFN_STAGE_SLOTS)
    epilogue_chunk(n_chunks - 1, (n_chunks - 1) % FFN_STAGE_SLOTS)


def _ffn_up(x1b, up_b, conv, conv_b, tm=1024, tn=D_FF // 2):
    s = x1b.shape[0]
    n_row_tiles = s // tm
    n_col = D_FF // tn
    halo_per_tile = tm // FFN_HALO
    n_halo_blocks = s // FFN_HALO
    prev = lambda j, i: (jnp.maximum(i * halo_per_tile - 1, 0), 0)
    nxt = lambda j, i: (jnp.minimum((i + 1) * halo_per_tile, n_halo_blocks - 1), 0)
    wspec = lambda shape, off: pl.BlockSpec(shape, lambda j, i: (0, j + off), pipeline_mode=pl.Buffered(1))
    return pl.pallas_call(
        functools.partial(_ffn_up_kernel, tm=tm, tn=tn, n_row_tiles=n_row_tiles),
        grid=(n_col, n_row_tiles),
        in_specs=[
            pl.BlockSpec((FFN_HALO, D_MODEL), prev),
            pl.BlockSpec((tm, D_MODEL), lambda j, i: (i, 0)),
            pl.BlockSpec((FFN_HALO, D_MODEL), nxt),
            wspec((D_MODEL, tn), 0), wspec((D_MODEL, tn), n_col),
            wspec((3, tn), 0), wspec((3, tn), n_col),
            wspec((1, tn), 0), wspec((1, tn), n_col),
        ],
        out_specs=pl.BlockSpec((tm, tn), lambda j, i: (i, j)),
        out_shape=jax.ShapeDtypeStruct((s, D_FF), jnp.bfloat16),
        scratch_shapes=[pltpu.VMEM((tm + 2 * FFN_HALO, D_MODEL), jnp.bfloat16),
                        pltpu.VMEM((FFN_STAGE_SLOTS, MXU_COLS // LANES, tm + 2 * FFN_HALO, LANES), jnp.float32),
                        pltpu.VMEM((FFN_STAGE_SLOTS, MXU_COLS // LANES, tm + 2 * FFN_HALO, LANES), jnp.float32)],
        compiler_params=_cparams("arbitrary", "arbitrary"),
        name="ffn_up",
    )(x1b, x1b, x1b, up_b, up_b, conv, conv, conv_b, conv_b)


SUBLANES = 8


def _zero_after(x):
    rows, cols = x.shape
    bits = pltpu.bitcast(x, jnp.uint32)
    acc = bits[:SUBLANES]
    for r in range(SUBLANES, rows, SUBLANES):
        acc = acc | bits[r:r + SUBLANES]
    tile = acc[:, :LANES]
    for c in range(LANES, cols, LANES):
        tile = tile | acc[:, c:c + LANES]
    zero = pltpu.bitcast((tile >> 16) >> 16, jnp.float32)
    return pltpu.repeat(pltpu.repeat(zero, rows // SUBLANES, axis=0), cols // LANES, axis=1)


def _ffn_down_ln2_kernel(r_ref, h_ref, wd_ref, g_ref, b_ref, o_ref, acc_ref, sum_ref, *, n_row_tiles):
    i = pl.program_id(0)
    k = pl.program_id(1)
    partial = lambda: jnp.dot(h_ref[...], wd_ref[...], preferred_element_type=jnp.float32)
    norm_previous = lambda: _layer_norm(sum_ref[...], g_ref[...], b_ref[...])

    @pl.when((i == 0) & (k == 0))
    def _():
        sum_ref[...] = jnp.zeros_like(sum_ref)

    @pl.when((k == 0) & (i < n_row_tiles))
    def _():
        y = norm_previous()
        o_ref[...] = y
        acc_ref[...] = (r_ref[...] + _zero_after(y)) + partial()

    @pl.when((k == 0) & (i == n_row_tiles))
    def _():
        o_ref[...] = norm_previous()

    @pl.when((k == 1) & (i < n_row_tiles))
    def _():
        sum_ref[...] = acc_ref[...] + partial()


def _ffn_down_ln2(r, h, wd_b, ln_g, ln_b, tm=512):
    s = r.shape[0]
    n = s // tm
    tk = D_FF // 2
    assert tk % MXU_COLS == 0
    cur = lambda i: jnp.minimum(i, n - 1)
    return pl.pallas_call(
        functools.partial(_ffn_down_ln2_kernel, n_row_tiles=n),
        grid=(n + 1, 2),
        in_specs=[
            pl.BlockSpec((tm, D_MODEL), lambda i, k: (cur(i), 0)),
            pl.BlockSpec((tm, tk), lambda i, k: (cur(i), k)),
            pl.BlockSpec((tk, D_MODEL), lambda i, k: (k, 0)),
            _resident((1, D_MODEL)), _resident((1, D_MODEL)),
        ],
        out_specs=pl.BlockSpec((tm, D_MODEL), lambda i, k: (jnp.maximum(i - 1, 0), 0)),
        out_shape=jax.ShapeDtypeStruct((s, D_MODEL), jnp.float32),
        scratch_shapes=[pltpu.VMEM((tm, D_MODEL), jnp.float32), pltpu.VMEM((tm, D_MODEL), jnp.float32)],
        compiler_params=_cparams("arbitrary", "arbitrary"),
        name="ffn_down_ln2",
    )(r, h, wd_b, ln_g, ln_b)


CAST_IN_KERNEL = ("fourier_w", "natten_w", "w_out", "ple_gate", "ple_proj", "ffn_up", "ffn_down")


def _encoder_layer(x, p, wts, bias, tables):
    pending = [name for name in CAST_IN_KERNEL if wts[name].dtype != jnp.bfloat16]
    z, gates, cast = _in_proj(x, wts["w_in"], wts["gate_b"], [wts[name] for name in pending])
    wts = {**wts, **dict(zip(pending, cast))}
    return _layer_after_in_proj(x, z, gates, p, wts, bias, tables), wts


def _layer_after_in_proj(x, z, gates, p, wts, bias, tables):
    fmix = _fourier_mix(z, tables)
    att = _natten(z, bias)
    r, x1b = _mix_ln1(x, fmix, att, gates, p, wts["fourier_w"], wts["natten_w"], wts["w_out"],
                      wts["ple_gate"], wts["ple_proj"], wts["ln1_g"], wts["ln1_b"])
    h = _ffn_up(x1b, wts["ffn_up"], wts["ffn_conv"], wts["ffn_conv_b"])
    return _ffn_down_ln2(r, h, wts["ffn_down"], wts["ln2_g"], wts["ln2_b"])


def kernel(x_prompt, x_sample, p_prompt, p_sample, w_in, gate_b, fourier_w, natten_rpb, natten_w, w_out,
           ln1_g, ln1_b, ffn_up, ffn_conv, ffn_conv_b, ffn_down, ple_proj, ple_gate, ln2_g, ln2_b):
    assert w_in.shape[0] == DEPTH
    xp, xs = x_prompt[0], x_sample[0]
    bf = lambda w: w.astype(jnp.bfloat16)
    row = lambda v: v.reshape(1, -1)
    for i in range(DEPTH):
        wts = {
            "w_in": bf(w_in[i]), "gate_b": row(gate_b[i]),
            "fourier_w": fourier_w[i], "natten_w": natten_w[i], "w_out": w_out[i],
            "ln1_g": row(ln1_g[i]), "ln1_b": row(ln1_b[i]),
            "ffn_up": ffn_up[i], "ffn_conv": ffn_conv[i], "ffn_conv_b": row(ffn_conv_b[i]),
            "ffn_down": ffn_down[i], "ple_proj": ple_proj[i], "ple_gate": ple_gate[i],
            "ln2_g": row(ln2_g[i]), "ln2_b": row(ln2_b[i]),
        }
        bias = _natten_bias(natten_rpb[i])
        xp, wts = _encoder_layer(xp, p_prompt[i, 0], wts, bias, _dft_tables(xp.shape[0]))
        xs, wts = _encoder_layer(xs, p_sample[i, 0], wts, bias, _dft_tables(xs.shape[0]))
    return (xp[None], xs[None])
```

```python
import functools
import math

import jax
import jax.numpy as jnp
import numpy as np
from jax import lax
from jax.experimental import pallas as pl
from jax.experimental.pallas import tpu as pltpu

D_MODEL = 2048
GRID_W = 64
PLE_DIM = 256
F_WIDTH = 1024
F_GROUPS = 8
F_GROUP_CH = F_WIDTH // F_GROUPS
NA_HEADS = 8
NA_HEAD_DIM = 128
NA_WIDTH = NA_HEADS * NA_HEAD_DIM
WIN_H = 8
WIN_W = 16
D_FF = 5632
IN_COLS = F_WIDTH + 3 * NA_WIDTH + 2 * D_MODEL
LN_EPS = 1e-5
DEPTH = 1
DN_ALPHA = (2.0 * DEPTH) ** 0.25

VMEM_PHYSICAL_BYTES = 64 * 1024 * 1024
VMEM_LIMIT_BYTES = 56 * 1024 * 1024
VMEM_INTERNAL_SCRATCH_BYTES = 6 * 1024 * 1024
NEG_BIAS = -1e30

DFT_N2 = 128
NA_QROWS = 8
NA_KROWS = 16
NA_KROW_OFFSET = 4


def _cparams(*semantics, vmem_limit_bytes=VMEM_LIMIT_BYTES):
    return pltpu.CompilerParams(dimension_semantics=semantics, vmem_limit_bytes=vmem_limit_bytes)


MXU_COLS = 256


CAST_ROW_ALIGN = 16


def _cast_rows_per_step(rows, n_steps):
    per_step = -(-rows // n_steps)
    per_step = -(-per_step // CAST_ROW_ALIGN) * CAST_ROW_ALIGN
    while rows % per_step:
        per_step += CAST_ROW_ALIGN
    return per_step


def _in_proj_kernel(*refs, tn, n_cast):
    x_ref, wz_ref, wg_ref, gb_ref = refs[:4]
    cast_in = refs[4:4 + n_cast]
    z_ref, g_ref = refs[4 + n_cast:6 + n_cast]
    cast_out = refs[6 + n_cast:6 + 2 * n_cast]
    xb_ref = refs[-1]

    @pl.when(pl.program_id(1) == 0)
    def _():
        xb_ref[...] = x_ref[...].astype(jnp.bfloat16)

    for src_ref, dst_ref in zip(cast_in, cast_out):
        dst_ref[...] = src_ref[...].astype(dst_ref.dtype)

    xb = xb_ref[...]
    for c in range(0, tn, MXU_COLS):
        sl = slice(c, c + MXU_COLS)
        acc = jnp.dot(xb, wg_ref[:, sl], preferred_element_type=jnp.float32)
        g_ref[:, sl] = jax.nn.sigmoid(acc + gb_ref[:, sl]).astype(g_ref.dtype)
    for c in range(0, tn, MXU_COLS):
        sl = slice(c, c + MXU_COLS)
        z_ref[:, sl] = jnp.dot(xb, wz_ref[:, sl], preferred_element_type=jnp.float32).astype(z_ref.dtype)


def _in_proj(x, w_in_b, gate_b, cast_weights=(), tm=1024, tn=1024):
    s = x.shape[0]
    z_cols = F_WIDTH + 3 * NA_WIDTH
    g_cols = IN_COLS - z_cols
    assert z_cols == g_cols
    n_col = z_cols // tn
    n_steps = (s // tm) * n_col
    cast_rows = [_cast_rows_per_step(w.shape[0], n_steps) for w in cast_weights]
    cast_blocks = tuple(w.shape[0] // r for w, r in zip(cast_weights, cast_rows))
    vmem_bytes = (2 * tm * D_MODEL * 4 + 2 * 2 * D_MODEL * tn * 2 + 2 * 2 * tm * tn * 2 + tm * D_MODEL * 2
                  + sum(2 * r * w.shape[1] * (4 + 2) for w, r in zip(cast_weights, cast_rows))
                  + VMEM_INTERNAL_SCRATCH_BYTES)
    assert vmem_bytes <= VMEM_PHYSICAL_BYTES - VMEM_INTERNAL_SCRATCH_BYTES // 2, vmem_bytes
    cast_specs = [
        pl.BlockSpec((r, w.shape[1]), lambda i, j, nb=nb: (jnp.minimum(i * n_col + j, nb - 1), 0))
        for w, r, nb in zip(cast_weights, cast_rows, cast_blocks)]
    outs = pl.pallas_call(
        functools.partial(_in_proj_kernel, tn=tn, n_cast=len(cast_weights)),
        grid=(s // tm, n_col),
        in_specs=[
            pl.BlockSpec((tm, D_MODEL), lambda i, j: (i, 0)),
            pl.BlockSpec((D_MODEL, tn), lambda i, j: (0, j)),
            pl.BlockSpec((D_MODEL, tn), lambda i, j: (0, j + n_col)),
            pl.BlockSpec((1, tn), lambda i, j: (0, j)),
        ] + cast_specs,
        out_specs=[pl.BlockSpec((tm, tn), lambda i, j: (i, j)),
                   pl.BlockSpec((tm, tn), lambda i, j: (i, j))] + cast_specs,
        out_shape=[jax.ShapeDtypeStruct((s, z_cols), jnp.bfloat16),
                   jax.ShapeDtypeStruct((s, g_cols), jnp.bfloat16)]
        + [jax.ShapeDtypeStruct(w.shape, jnp.bfloat16) for w in cast_weights],
        scratch_shapes=[pltpu.VMEM((tm, D_MODEL), jnp.bfloat16)],
        compiler_params=_cparams("arbitrary", "arbitrary", vmem_limit_bytes=vmem_bytes),
        name="in_proj",
    )(x, w_in_b, w_in_b, gate_b, *cast_weights)
    return outs[0], outs[1], tuple(outs[2:])


def _dft_tables(s):
    n2 = DFT_N2
    n1 = s // n2
    k1 = np.arange(n1)
    w1 = np.exp(-2j * np.pi * np.outer(k1, k1) / n1) / math.sqrt(n1)
    fa = np.concatenate([w1.real, w1.imag], axis=0)
    k2 = np.arange(n2)
    w2 = np.exp(-2j * np.pi * np.outer(k2, k2) / n2) / math.sqrt(n2)
    tw = np.exp(-2j * np.pi * np.outer(k1, k2) / s)
    c = np.arange(F_GROUP_CH)
    wc = np.exp(-2j * np.pi * np.outer(c, c) / F_GROUP_CH) / math.sqrt(F_GROUP_CH)
    wch = np.concatenate([wc.real, -wc.imag], axis=0)
    f32 = lambda a: jnp.asarray(a, dtype=jnp.float32)
    return (f32(fa), f32(w2.real), f32(w2.imag), f32(tw.real), f32(tw.imag), f32(wch))


DFT_SLABS = 16


def _dft_pos_a_kernel(fa_ref, x_ref, o_ref, *, n1):
    fa = fa_ref[...].astype(jnp.bfloat16)
    xt = jnp.swapaxes(x_ref[...], 0, 1)
    re, im = [], []
    for j in range(DFT_SLABS):
        p = jnp.dot(fa, xt[j], preferred_element_type=jnp.float32)
        re.append(p[:n1].astype(o_ref.dtype))
        im.append(p[n1:].astype(o_ref.dtype))
    o_ref[0] = jnp.swapaxes(jnp.stack(re, axis=0), 0, 1)
    o_ref[1] = jnp.swapaxes(jnp.stack(im, axis=0), 0, 1)


def _dft_pos_a(z3, fa):
    n1, n2, _ = z3.shape
    return pl.pallas_call(
        functools.partial(_dft_pos_a_kernel, n1=n1),
        grid=(n2 // DFT_SLABS,),
        in_specs=[
            pl.BlockSpec((2 * n1, n1), lambda i: (0, 0)),
            pl.BlockSpec((n1, DFT_SLABS, F_WIDTH), lambda i: (0, i, 0)),
        ],
        out_specs=pl.BlockSpec((2, n1, DFT_SLABS, F_WIDTH), lambda i: (0, 0, i, 0)),
        out_shape=jax.ShapeDtypeStruct((2, n1, n2, F_WIDTH), jnp.bfloat16),
        compiler_params=_cparams("parallel"),
        name="dft_pos_a",
    )(fa, z3)


def _dft_pos_c_kernel(w2r_ref, w2i_ref, twr_ref, twi_ref, wch_ref, a_ref, o_ref):
    w2r = w2r_ref[...]
    w2i = w2i_ref[...]
    wch = wch_ref[...].astype(jnp.bfloat16)
    def position_dft(j):
        tr = twr_ref[j:j + 1, :]
        ti = twi_ref[j:j + 1, :]
        gr = (w2r * tr - w2i * ti).astype(jnp.bfloat16)
        gi = (w2r * ti + w2i * tr).astype(jnp.bfloat16)
        g4 = jnp.concatenate(
            [jnp.concatenate([gr, -gi], axis=1), jnp.concatenate([gi, gr], axis=1)], axis=0)
        a = jnp.concatenate([a_ref[0, j], a_ref[1, j]], axis=0)
        return jnp.dot(g4, a, preferred_element_type=jnp.float32).astype(jnp.bfloat16)

    def channel_dft(y):
        n2 = y.shape[0] // 2
        yr, yi = y[:n2], y[n2:]
        groups = []
        for g in range(F_GROUPS):
            sl = slice(g * F_GROUP_CH, (g + 1) * F_GROUP_CH)
            lhs = jnp.concatenate([yr[:, sl], yi[:, sl]], axis=1)
            groups.append(jnp.dot(lhs, wch, preferred_element_type=jnp.float32).astype(o_ref.dtype))
        return jnp.concatenate(groups, axis=1)

    slabs = []
    y_next = position_dft(0)
    for j in range(DFT_SLABS):
        y = y_next
        if j + 1 < DFT_SLABS:
            y_next = position_dft(j + 1)
        slabs.append(channel_dft(y))
    o_ref[...] = jnp.swapaxes(jnp.stack(slabs, axis=0), 0, 1)


def _dft_pos_c(a, w2r, w2i, twr, twi, wch):
    _, n1, n2, c = a.shape
    full = lambda shape: pl.BlockSpec(shape, lambda i: (0,) * len(shape))
    return pl.pallas_call(
        _dft_pos_c_kernel,
        grid=(n1 // DFT_SLABS,),
        in_specs=[
            full((n2, n2)), full((n2, n2)),
            pl.BlockSpec((DFT_SLABS, n2), lambda i: (i, 0)),
            pl.BlockSpec((DFT_SLABS, n2), lambda i: (i, 0)),
            full((2 * F_GROUP_CH, F_GROUP_CH)),
            pl.BlockSpec((2, DFT_SLABS, n2, c), lambda i: (0, i, 0, 0)),
        ],
        out_specs=pl.BlockSpec((n2, DFT_SLABS, c), lambda i: (0, i, 0)),
        out_shape=jax.ShapeDtypeStruct((n2, n1, c), jnp.bfloat16),
        compiler_params=_cparams("parallel"),
        name="dft_pos_c",
    )(w2r, w2i, twr, twi, wch, a)


def _fourier_mix(z, tables):
    s = z.shape[0]
    n2 = DFT_N2
    n1 = s // n2
    fa, w2r, w2i, twr, twi, wch = tables
    a = _dft_pos_a(z.reshape(n1, n2, z.shape[1]), fa)
    y = _dft_pos_c(a, w2r, w2i, twr, twi, wch)
    return y.reshape(s, F_WIDTH)


NA_RPB_H = 2 * WIN_H - 1
NA_RPB_W = 2 * WIN_W - 1
NA_KROW_PAIRS = NA_KROWS // 2


def _natten_row_windows():
    i = np.arange(NA_QROWS)
    mid = np.full(NA_QROWS, -(WIN_H // 2))
    first = np.where(i < WIN_H // 2, -i, mid)
    last = np.where(i > WIN_H // 2, -i, mid)
    return (first, mid, last)


def _natten_bias_kernel(rpb_ref, o_ref, cb_ref):
    h = pl.program_id(0)
    shape = (GRID_W, 2 * GRID_W)
    qc = lax.broadcasted_iota(jnp.int32, shape, 0)
    lane = lax.broadcasted_iota(jnp.int32, shape, 1)
    kc = lane & (GRID_W - 1)
    col_start = jnp.clip(qc - WIN_W // 2, 0, GRID_W - WIN_W)
    col_ok = (kc >= col_start) & (kc < col_start + WIN_W)
    dc = jnp.where(col_ok, kc - qc + WIN_W - 1, -1)
    neg = jnp.full(shape, NEG_BIAS, jnp.float32)
    for d in range(NA_RPB_H):
        tile = neg
        for t in range(NA_RPB_W):
            r = rpb_ref[(h * NA_RPB_H + d) * NA_RPB_W + t] * (NA_HEAD_DIM ** 0.5)
            tile = jnp.where(dc == t, r, tile)
        cb_ref[d] = tile
    left = lane < GRID_W
    for v, lo in enumerate(_natten_row_windows()):
        for i in range(NA_QROWS):
            for jj in range(NA_KROW_PAIRS):
                dr = (2 * jj - NA_KROW_OFFSET - i, 2 * jj + 1 - NA_KROW_OFFSET - i)
                ok = [lo[i] <= d <= lo[i] + WIN_H - 1 for d in dr]
                halves = [cb_ref[d + WIN_H - 1] if good else neg for d, good in zip(dr, ok)]
                tile = jnp.where(left, halves[0], halves[1]) if any(ok) else neg
                o_ref[v, 0, i * GRID_W:(i + 1) * GRID_W, jj * 2 * GRID_W:(jj + 1) * 2 * GRID_W] = tile


def _natten_bias(rpb):
    blk = NA_QROWS * GRID_W
    keys = NA_KROWS * GRID_W
    return pl.pallas_call(
        _natten_bias_kernel,
        grid=(NA_HEADS,),
        in_specs=[pl.BlockSpec(memory_space=pltpu.SMEM)],
        out_specs=pl.BlockSpec((3, 1, blk, keys), lambda h: (0, h, 0, 0)),
        out_shape=jax.ShapeDtypeStruct((3, NA_HEADS, blk, keys), jnp.float32),
        scratch_shapes=[pltpu.VMEM((NA_RPB_H, GRID_W, 2 * GRID_W), jnp.float32)],
        compiler_params=_cparams("parallel"),
        name="natten_bias",
    )(rpb.astype(jnp.float32).reshape(-1))


NA_HEADS_PER_STEP = 4


NA_ROW_BLOCK = 32


def _natten_kernel(q_ref, kp_ref, kc_ref, kn_ref, vp_ref, vc_ref, vn_ref, b_ref, o_ref, t_ref, p_ref):
    half = (NA_KROWS - NA_QROWS) // 2 * GRID_W
    blk = NA_QROWS * GRID_W
    nq = blk // 2
    nkeys = blk + half
    exp2_scale = (NA_HEAD_DIM ** -0.5) * math.log2(math.e)
    ones = jnp.ones((nkeys, NA_HEAD_DIM), jnp.bfloat16)
    chains = []
    for h in range(NA_HEADS_PER_STEP):
        sl = slice(h * NA_HEAD_DIM, (h + 1) * NA_HEAD_DIM)
        chains.append((h, sl, 0, 0, (kp_ref, kc_ref), (vp_ref, vc_ref), (slice(blk - half, blk), slice(None))))
        chains.append((h, sl, nq, half, (kc_ref, kn_ref), (vc_ref, vn_ref), (slice(None), slice(0, half))))

    def scores(ch):
        h, sl, row0, key0, k_refs, v_refs, parts = chains[ch]
        kwin = jnp.concatenate([r[p, sl] for r, p in zip(k_refs, parts)], axis=0)
        t_ref[ch] = lax.dot_general(q_ref[row0:row0 + nq, sl], kwin, (((1,), (1,)), ((), ())),
                                    preferred_element_type=jnp.float32)

    def softmax(ch):
        h, sl, row0, key0, k_refs, v_refs, parts = chains[ch]
        biased = lambda r0: (t_ref[ch, r0:r0 + NA_ROW_BLOCK, :]
                             + b_ref[0, h, row0 + r0:row0 + r0 + NA_ROW_BLOCK, key0:key0 + nkeys])
        row_blocks = range(0, nq, NA_ROW_BLOCK)
        maxes = [jnp.max(biased(r0), axis=-1, keepdims=True) for r0 in row_blocks]
        for m, r0 in zip(maxes, row_blocks):
            p_ref[ch, r0:r0 + NA_ROW_BLOCK, :] = jnp.exp2((biased(r0) - m) * exp2_scale).astype(p_ref.dtype)

    def weighted_values(ch):
        h, sl, row0, key0, k_refs, v_refs, parts = chains[ch]
        vwin = jnp.concatenate([r[p, sl] for r, p in zip(v_refs, parts)], axis=0)
        ol = jnp.dot(p_ref[ch], jnp.concatenate([vwin, ones], axis=1), preferred_element_type=jnp.float32)
        o_ref[row0:row0 + nq, sl] = (ol[:, :NA_HEAD_DIM] / ol[:, NA_HEAD_DIM:]).astype(o_ref.dtype)

    scores(0)
    for ch in range(len(chains)):
        if ch + 1 < len(chains):
            scores(ch + 1)
        softmax(ch)
        weighted_values(ch)


def _natten(z, bias):
    s = z.shape[0]
    blk = NA_QROWS * GRID_W
    nb = s // blk
    width = NA_HEADS_PER_STEP * NA_HEAD_DIM
    q0 = F_WIDTH // width
    k0 = q0 + NA_WIDTH // width
    v0 = k0 + NA_WIDTH // width
    prev = lambda b: jnp.maximum(b - 1, 0)
    nxt = lambda b: jnp.minimum(b + 1, nb - 1)
    variant = lambda b: jnp.where(b == 0, 0, jnp.where(b == nb - 1, 2, 1))
    spec = lambda col0, rowfn: pl.BlockSpec((blk, width), lambda h, b: (rowfn(b), col0 + h))
    same = lambda b: b
    return pl.pallas_call(
        _natten_kernel,
        grid=(NA_HEADS // NA_HEADS_PER_STEP, nb),
        in_specs=[
            spec(q0, same),
            spec(k0, prev), spec(k0, same), spec(k0, nxt),
            spec(v0, prev), spec(v0, same), spec(v0, nxt),
            pl.BlockSpec((1, NA_HEADS_PER_STEP, blk, NA_KROWS * GRID_W), lambda h, b: (variant(b), h, 0, 0)),
        ],
        out_specs=pl.BlockSpec((blk, width), lambda h, b: (b, h)),
        out_shape=jax.ShapeDtypeStruct((s, NA_WIDTH), jnp.bfloat16),
        scratch_shapes=[pltpu.VMEM((2 * NA_HEADS_PER_STEP, blk // 2, blk + blk // 2), jnp.float32),
                        pltpu.VMEM((2 * NA_HEADS_PER_STEP, blk // 2, blk + blk // 2), jnp.bfloat16)],
        compiler_params=_cparams("parallel", "arbitrary"),
        name="natten",
    )(z, z, z, z, z, z, z, bias)


def _layer_norm(h, g, b):
    mu = jnp.mean(h, axis=-1, keepdims=True)
    d = h - mu
    var = jnp.mean(d * d, axis=-1, keepdims=True)
    return d * lax.rsqrt(var + LN_EPS) * g + b


def _mix_ln1_kernel(x_ref, fm_ref, at_ref, ga_ref, gb_ref, p_ref, fw_ref, nw_ref, wo_ref, pg_ref, pp_ref,
                    g_ref, b_ref, r_ref, x1b_ref, h_ref):
    @pl.when(pl.program_id(0) == 0)
    def _():
        h_ref[...] = jnp.zeros_like(h_ref)

    a = jnp.dot(fm_ref[...], fw_ref[...], preferred_element_type=jnp.float32)
    b = jnp.dot(at_ref[...], nw_ref[...], preferred_element_type=jnp.float32)
    emb = jnp.dot(p_ref[...].astype(jnp.bfloat16), pp_ref[...], preferred_element_type=jnp.float32)

    x1 = _layer_norm(h_ref[...], g_ref[...], b_ref[...])
    x1b = x1.astype(jnp.bfloat16)
    gate = jax.nn.sigmoid(jnp.dot(x1b, pg_ref[...], preferred_element_type=jnp.float32))
    r_ref[...] = DN_ALPHA * x1 + gate * emb
    x1b_ref[...] = x1b

    merged = ga_ref[...].astype(jnp.float32) * a + gb_ref[...].astype(jnp.float32) * b
    h_ref[...] = DN_ALPHA * x_ref[...] + jnp.dot(merged.astype(jnp.bfloat16), wo_ref[...],
                                                 preferred_element_type=jnp.float32)


def _resident(shape):
    return pl.BlockSpec(shape, lambda *_: (0,) * len(shape), pipeline_mode=pl.Buffered(1))


def _mix_ln1(x, fmix, att, gates, p, fw_b, nw_b, wo_b, pg_b, pp_b, ln_g, ln_b, tm=256):
    s = x.shape[0]
    n = s // tm
    cur = lambda width, col=0: pl.BlockSpec((tm, width), lambda i: (jnp.minimum(i, n - 1), col))
    lag = lambda width: pl.BlockSpec((tm, width), lambda i: (jnp.maximum(i - 1, 0), 0))
    return pl.pallas_call(
        _mix_ln1_kernel,
        grid=(n + 1,),
        in_specs=[
            cur(D_MODEL), cur(F_WIDTH), cur(NA_WIDTH), cur(D_MODEL, 0), cur(D_MODEL, 1), lag(PLE_DIM),
            _resident((F_WIDTH, D_MODEL)), _resident((NA_WIDTH, D_MODEL)), _resident((D_MODEL, D_MODEL)),
            _resident((D_MODEL, D_MODEL)), _resident((PLE_DIM, D_MODEL)),
            _resident((1, D_MODEL)), _resident((1, D_MODEL)),
        ],
        out_specs=[lag(D_MODEL), lag(D_MODEL)],
        out_shape=[jax.ShapeDtypeStruct((s, D_MODEL), jnp.float32),
                   jax.ShapeDtypeStruct((s, D_MODEL), jnp.bfloat16)],
        scratch_shapes=[pltpu.VMEM((tm, D_MODEL), jnp.float32)],
        compiler_params=_cparams("arbitrary"),
        name="mix_ln1",
    )(x, fmix, att, gates, gates, p, fw_b, nw_b, wo_b, pg_b, pp_b, ln_g, ln_b)


FFN_HALO = 16


def _gelu_tanh(u):
    return 0.5 * u * (1.0 + jnp.tanh(math.sqrt(2.0 / math.pi) * (u + 0.044715 * (u * u * u))))


LANES = 128
FFN_ROW_BLOCK = 128
FFN_STAGE_SLOTS = 3


def _ffn_up_kernel(xp_ref, xc_ref, xn_ref, wa_ref, wb_ref, ca_ref, cb_ref, ba_ref, bb_ref, h_ref,
                   xe_ref, ua_ref, ub_ref, *, tm, tn, n_row_tiles):
    i = pl.program_id(1)
    xe_ref[:FFN_HALO, :] = jnp.where(i > 0, xp_ref[...], jnp.zeros_like(xp_ref))
    xe_ref[FFN_HALO:FFN_HALO + tm, :] = xc_ref[...]
    xe_ref[FFN_HALO + tm:, :] = jnp.where(i < n_row_tiles - 1, xn_ref[...], jnp.zeros_like(xn_ref))
    lanes_per_chunk = MXU_COLS // LANES

    def matmul_chunk(c, slot):
        sl = slice(c * MXU_COLS, (c + 1) * MXU_COLS)
        for w_ref, u_ref in ((wa_ref, ua_ref), (wb_ref, ub_ref)):
            u = jnp.dot(xe_ref[...], w_ref[:, sl], preferred_element_type=jnp.float32)
            for k in range(lanes_per_chunk):
                u_ref[slot, k] = u[:, k * LANES:(k + 1) * LANES]

    def epilogue_chunk(c, slot):
        for k in range(lanes_per_chunk):
            sl = slice(c * MXU_COLS + k * LANES, c * MXU_COLS + (k + 1) * LANES)
            taps = []
            for c_ref, b_ref in ((ca_ref, ba_ref), (cb_ref, bb_ref)):
                taps.append([jnp.broadcast_to(c_ref[t:t + 1, sl], (FFN_ROW_BLOCK, LANES)) for t in range(3)]
                            + [jnp.broadcast_to(b_ref[:, sl], (FFN_ROW_BLOCK, LANES))])
            for r in range(0, tm, FFN_ROW_BLOCK):
                halves = []
                for u_ref, (c0, c1, c2, bias) in zip((ua_ref, ub_ref), taps):
                    lo = u_ref[slot, k, pl.ds(FFN_HALO - 1 + r, FFN_ROW_BLOCK), :]
                    mid = u_ref[slot, k, pl.ds(FFN_HALO + r, FFN_ROW_BLOCK), :]
                    hi = u_ref[slot, k, pl.ds(FFN_HALO + 1 + r, FFN_ROW_BLOCK), :]
                    halves.append(lo * c0 + mid * c1 + hi * c2 + bias)
                h_ref[pl.ds(r, FFN_ROW_BLOCK), sl] = (_gelu_tanh(halves[0]) * halves[1]).astype(h_ref.dtype)

    n_chunks = tn // MXU_COLS
    for c in range(n_chunks):
        matmul_chunk(c, c % FFN_STAGE_SLOTS)
        if c > 0:
            epilogue_chunk(c - 1, (c - 1) % FFN_STAGE_SLOTS)
    epilogue_chunk(n_chunks - 1, (n_chunks - 1) % FFN_STAGE_SLOTS)


def _ffn_up(x1b, up_b, conv, conv_b, tm=1024, tn=D_FF // 2):
    s = x1b.shape[0]
    n_row_tiles = s // tm
    n_col = D_FF // tn
    halo_per_tile = tm // FFN_HALO
    n_halo_blocks = s // FFN_HALO
    prev = lambda j, i: (jnp.maximum(i * halo_per_tile - 1, 0), 0)
    nxt = lambda j, i: (jnp.minimum((i + 1) * halo_per_tile, n_halo_blocks - 1), 0)
    wspec = lambda shape, off: pl.BlockSpec(shape, lambda j, i: (0, j + off), pipeline_mode=pl.Buffered(1))
    return pl.pallas_call(
        functools.partial(_ffn_up_kernel, tm=tm, tn=tn, n_row_tiles=n_row_tiles),
        grid=(n_col, n_row_tiles),
        in_specs=[
            pl.BlockSpec((FFN_HALO, D_MODEL), prev),
            pl.BlockSpec((tm, D_MODEL), lambda j, i: (i, 0)),
            pl.BlockSpec((FFN_HALO, D_MODEL), nxt),
            wspec((D_MODEL, tn), 0), wspec((D_MODEL, tn), n_col),
            wspec((3, tn), 0), wspec((3, tn), n_col),
            wspec((1, tn), 0), wspec((1, tn), n_col),
        ],
        out_specs=pl.BlockSpec((tm, tn), lambda j, i: (i, j)),
        out_shape=jax.ShapeDtypeStruct((s, D_FF), jnp.bfloat16),
        scratch_shapes=[pltpu.VMEM((tm + 2 * FFN_HALO, D_MODEL), jnp.bfloat16),
                        pltpu.VMEM((FFN_STAGE_SLOTS, MXU_COLS // LANES, tm + 2 * FFN_HALO, LANES), jnp.float32),
                        pltpu.VMEM((FFN_STAGE_SLOTS, MXU_COLS // LANES, tm + 2 * FFN_HALO, LANES), jnp.float32)],
        compiler_params=_cparams("arbitrary", "arbitrary"),
        name="ffn_up",
    )(x1b, x1b, x1b, up_b, up_b, conv, conv, conv_b, conv_b)


SUBLANES = 8


def _zero_after(x):
    rows, cols = x.shape
    bits = pltpu.bitcast(x, jnp.uint32)
    acc = bits[:SUBLANES]
    for r in range(SUBLANES, rows, SUBLANES):
        acc = acc | bits[r:r + SUBLANES]
    tile = acc[:, :LANES]
    for c in range(LANES, cols, LANES):
        tile = tile | acc[:, c:c + LANES]
    zeros = pltpu.bitcast((tile >> 16) >> 16, jnp.float32)
    column = jnp.concatenate([zeros] * (rows // SUBLANES), axis=0)
    return jnp.concatenate([column] * (cols // LANES), axis=1)


def _ffn_down_ln2_kernel(r_ref, h_ref, wd_ref, g_ref, b_ref, o_ref, acc_ref, sum_ref, *, n_row_tiles):
    i = pl.program_id(0)
    k = pl.program_id(1)
    partial = lambda: jnp.dot(h_ref[...], wd_ref[...], preferred_element_type=jnp.float32)
    norm_previous = lambda: _layer_norm(sum_ref[...], g_ref[...], b_ref[...])

    @pl.when((i == 0) & (k == 0))
    def _():
        sum_ref[...] = jnp.zeros_like(sum_ref)

    @pl.when((k == 0) & (i < n_row_tiles))
    def _():
        y = norm_previous()
        o_ref[...] = y
        acc_ref[...] = (r_ref[...] + _zero_after(y)) + partial()

    @pl.when((k == 0) & (i == n_row_tiles))
    def _():
        o_ref[...] = norm_previous()

    @pl.when((k == 1) & (i < n_row_tiles))
    def _():
        sum_ref[...] = acc_ref[...] + partial()


def _ffn_down_ln2(r, h, wd_b, ln_g, ln_b, tm=512):
    s = r.shape[0]
    n = s // tm
    tk = D_FF // 2
    assert tk % MXU_COLS == 0
    cur = lambda i: jnp.minimum(i, n - 1)
    return pl.pallas_call(
        functools.partial(_ffn_down_ln2_kernel, n_row_tiles=n),
        grid=(n + 1, 2),
        in_specs=[
            pl.BlockSpec((tm, D_MODEL), lambda i, k: (cur(i), 0)),
            pl.BlockSpec((tm, tk), lambda i, k: (cur(i), k)),
            pl.BlockSpec((tk, D_MODEL), lambda i, k: (k, 0)),
            _resident((1, D_MODEL)), _resident((1, D_MODEL)),
        ],
        out_specs=pl.BlockSpec((tm, D_MODEL), lambda i, k: (jnp.maximum(i - 1, 0), 0)),
        out_shape=jax.ShapeDtypeStruct((s, D_MODEL), jnp.float32),
        scratch_shapes=[pltpu.VMEM((tm, D_MODEL), jnp.float32), pltpu.VMEM((tm, D_MODEL), jnp.float32)],
        compiler_params=_cparams("arbitrary", "arbitrary"),
        name="ffn_down_ln2",
    )(r, h, wd_b, ln_g, ln_b)


CAST_IN_KERNEL = ("fourier_w", "natten_w", "w_out", "ple_gate", "ple_proj", "ffn_up", "ffn_down")


def _encoder_layer(x, p, wts, bias, tables):
    pending = [name for name in CAST_IN_KERNEL if wts[name].dtype != jnp.bfloat16]
    z, gates, cast = _in_proj(x, wts["w_in"], wts["gate_b"], [wts[name] for name in pending])
    wts = {**wts, **dict(zip(pending, cast))}
    return _layer_after_in_proj(x, z, gates, p, wts, bias, tables), wts


def _layer_after_in_proj(x, z, gates, p, wts, bias, tables):
    fmix = _fourier_mix(z, tables)
    att = _natten(z, bias)
    r, x1b = _mix_ln1(x, fmix, att, gates, p, wts["fourier_w"], wts["natten_w"], wts["w_out"],
                      wts["ple_gate"], wts["ple_proj"], wts["ln1_g"], wts["ln1_b"])
    h = _ffn_up(x1b, wts["ffn_up"], wts["ffn_conv"], wts["ffn_conv_b"])
    return _ffn_down_ln2(r, h, wts["ffn_down"], wts["ln2_g"], wts["ln2_b"])


def kernel(x_prompt, x_sample, p_prompt, p_sample, w_in, gate_b, fourier_w, natten_rpb, natten_w, w_out,
           ln1_g, ln1_b, ffn_up, ffn_conv, ffn_conv_b, ffn_down, ple_proj, ple_gate, ln2_g, ln2_b):
    assert w_in.shape[0] == DEPTH
    xp, xs = x_prompt[0], x_sample[0]
    bf = lambda w: w.astype(jnp.bfloat16)
    row = lambda v: v.reshape(1, -1)
    for i in range(DEPTH):
        wts = {
            "w_in": bf(w_in[i]), "gate_b": row(gate_b[i]),
            "fourier_w": fourier_w[i], "natten_w": natten_w[i], "w_out": w_out[i],
            "ln1_g": row(ln1_g[i]), "ln1_b": row(ln1_b[i]),
            "ffn_up": ffn_up[i], "ffn_conv": ffn_conv[i], "ffn_conv_b": row(ffn_conv_b[i]),
            "ffn_down": ffn_down[i], "ple_proj": ple_proj[i], "ple_gate": ple_gate[i],
            "ln2_g": row(ln2_g[i]), "ln2_b": row(ln2_b[i]),
        }
        bias = _natten_bias(natten_rpb[i])
        xp, wts = _encoder_layer(xp, p_prompt[i, 0], wts, bias, _dft_tables(xp.shape[0]))
        xs, wts = _encoder_layer(xs, p_sample[i, 0], wts, bias, _dft_tables(xs.shape[0]))
    return (xp[None], xs[None])
```

```python
import functools
import math

import jax
import jax.numpy as jnp
import numpy as np
from jax import lax
from jax.experimental import pallas as pl
from jax.experimental.pallas import tpu as pltpu

D_MODEL = 2048
GRID_W = 64
PLE_DIM = 256
F_WIDTH = 1024
F_GROUPS = 8
F_GROUP_CH = F_WIDTH // F_GROUPS
NA_HEADS = 8
NA_HEAD_DIM = 128
NA_WIDTH = NA_HEADS * NA_HEAD_DIM
WIN_H = 8
WIN_W = 16
D_FF = 5632
IN_COLS = F_WIDTH + 3 * NA_WIDTH + 2 * D_MODEL
LN_EPS = 1e-5
DEPTH = 1
DN_ALPHA = (2.0 * DEPTH) ** 0.25

VMEM_PHYSICAL_BYTES = 64 * 1024 * 1024
VMEM_LIMIT_BYTES = 56 * 1024 * 1024
VMEM_INTERNAL_SCRATCH_BYTES = 6 * 1024 * 1024
NEG_BIAS = -1e30

DFT_N2 = 128
NA_QROWS = 8
NA_KROWS = 16
NA_KROW_OFFSET = 4


def _cparams(*semantics, vmem_limit_bytes=VMEM_LIMIT_BYTES):
    return pltpu.CompilerParams(dimension_semantics=semantics, vmem_limit_bytes=vmem_limit_bytes)


MXU_COLS = 256


CAST_ROW_ALIGN = 16


def _cast_rows_per_step(rows, n_steps):
    per_step = -(-rows // n_steps)
    per_step = -(-per_step // CAST_ROW_ALIGN) * CAST_ROW_ALIGN
    while rows % per_step:
        per_step += CAST_ROW_ALIGN
    return per_step


def _in_proj_kernel(*refs, tn, n_cast):
    x_ref, wz_ref, wg_ref, gb_ref = refs[:4]
    cast_in = refs[4:4 + n_cast]
    z_ref, g_ref = refs[4 + n_cast:6 + n_cast]
    cast_out = refs[6 + n_cast:6 + 2 * n_cast]
    xb_ref = refs[-1]

    @pl.when(pl.program_id(1) == 0)
    def _():
        xb_ref[...] = x_ref[...].astype(jnp.bfloat16)

    for src_ref, dst_ref in zip(cast_in, cast_out):
        dst_ref[...] = src_ref[...].astype(dst_ref.dtype)

    xb = xb_ref[...]
    for c in range(0, tn, MXU_COLS):
        sl = slice(c, c + MXU_COLS)
        acc = jnp.dot(xb, wg_ref[:, sl], preferred_element_type=jnp.float32)
        g_ref[:, sl] = jax.nn.sigmoid(acc + gb_ref[:, sl]).astype(g_ref.dtype)
    for c in range(0, tn, MXU_COLS):
        sl = slice(c, c + MXU_COLS)
        z_ref[:, sl] = jnp.dot(xb, wz_ref[:, sl], preferred_element_type=jnp.float32).astype(z_ref.dtype)


def _in_proj(x, w_in_b, gate_b, cast_weights=(), tm=1024, tn=1024):
    s = x.shape[0]
    z_cols = F_WIDTH + 3 * NA_WIDTH
    g_cols = IN_COLS - z_cols
    assert z_cols == g_cols
    n_col = z_cols // tn
    n_steps = (s // tm) * n_col
    cast_rows = [_cast_rows_per_step(w.shape[0], n_steps) for w in cast_weights]
    cast_blocks = tuple(w.shape[0] // r for w, r in zip(cast_weights, cast_rows))
    vmem_bytes = (2 * tm * D_MODEL * 4 + 2 * 2 * D_MODEL * tn * 2 + 2 * 2 * tm * tn * 2 + tm * D_MODEL * 2
                  + sum(2 * r * w.shape[1] * (4 + 2) for w, r in zip(cast_weights, cast_rows))
                  + VMEM_INTERNAL_SCRATCH_BYTES)
    assert vmem_bytes <= VMEM_PHYSICAL_BYTES - VMEM_INTERNAL_SCRATCH_BYTES // 2, vmem_bytes
    cast_specs = [
        pl.BlockSpec((r, w.shape[1]), lambda i, j, nb=nb: (jnp.minimum(i * n_col + j, nb - 1), 0))
        for w, r, nb in zip(cast_weights, cast_rows, cast_blocks)]
    outs = pl.pallas_call(
        functools.partial(_in_proj_kernel, tn=tn, n_cast=len(cast_weights)),
        grid=(s // tm, n_col),
        in_specs=[
            pl.BlockSpec((tm, D_MODEL), lambda i, j: (i, 0)),
            pl.BlockSpec((D_MODEL, tn), lambda i, j: (0, j)),
            pl.BlockSpec((D_MODEL, tn), lambda i, j: (0, j + n_col)),
            pl.BlockSpec((1, tn), lambda i, j: (0, j)),
        ] + cast_specs,
        out_specs=[pl.BlockSpec((tm, tn), lambda i, j: (i, j)),
                   pl.BlockSpec((tm, tn), lambda i, j: (i, j))] + cast_specs,
        out_shape=[jax.ShapeDtypeStruct((s, z_cols), jnp.bfloat16),
                   jax.ShapeDtypeStruct((s, g_cols), jnp.bfloat16)]
        + [jax.ShapeDtypeStruct(w.shape, jnp.bfloat16) for w in cast_weights],
        scratch_shapes=[pltpu.VMEM((tm, D_MODEL), jnp.bfloat16)],
        compiler_params=_cparams("arbitrary", "arbitrary", vmem_limit_bytes=vmem_bytes),
        name="in_proj",
    )(x, w_in_b, w_in_b, gate_b, *cast_weights)
    return outs[0], outs[1], tuple(outs[2:])


def _dft_tables(s):
    n2 = DFT_N2
    n1 = s // n2
    k1 = np.arange(n1)
    w1 = np.exp(-2j * np.pi * np.outer(k1, k1) / n1) / math.sqrt(n1)
    fa = np.concatenate([w1.real, w1.imag], axis=0)
    k2 = np.arange(n2)
    w2 = np.exp(-2j * np.pi * np.outer(k2, k2) / n2) / math.sqrt(n2)
    tw = np.exp(-2j * np.pi * np.outer(k1, k2) / s)
    c = np.arange(F_GROUP_CH)
    wc = np.exp(-2j * np.pi * np.outer(c, c) / F_GROUP_CH) / math.sqrt(F_GROUP_CH)
    wch = np.concatenate([wc.real, -wc.imag], axis=0)
    f32 = lambda a: jnp.asarray(a, dtype=jnp.float32)
    return (f32(fa), f32(w2.real), f32(w2.imag), f32(tw.real), f32(tw.imag), f32(wch))


DFT_SLABS = 16


def _dft_pos_a_kernel(fa_ref, x_ref, o_ref, *, n1):
    fa = fa_ref[...].astype(jnp.bfloat16)
    xt = jnp.swapaxes(x_ref[...], 0, 1)
    re, im = [], []
    for j in range(DFT_SLABS):
        p = jnp.dot(fa, xt[j], preferred_element_type=jnp.float32)
        re.append(p[:n1].astype(o_ref.dtype))
        im.append(p[n1:].astype(o_ref.dtype))
    o_ref[0] = jnp.swapaxes(jnp.stack(re, axis=0), 0, 1)
    o_ref[1] = jnp.swapaxes(jnp.stack(im, axis=0), 0, 1)


def _dft_pos_a(z3, fa):
    n1, n2, _ = z3.shape
    return pl.pallas_call(
        functools.partial(_dft_pos_a_kernel, n1=n1),
        grid=(n2 // DFT_SLABS,),
        in_specs=[
            pl.BlockSpec((2 * n1, n1), lambda i: (0, 0)),
            pl.BlockSpec((n1, DFT_SLABS, F_WIDTH), lambda i: (0, i, 0)),
        ],
        out_specs=pl.BlockSpec((2, n1, DFT_SLABS, F_WIDTH), lambda i: (0, 0, i, 0)),
        out_shape=jax.ShapeDtypeStruct((2, n1, n2, F_WIDTH), jnp.bfloat16),
        compiler_params=_cparams("parallel"),
        name="dft_pos_a",
    )(fa, z3)


def _dft_pos_c_kernel(w2r_ref, w2i_ref, twr_ref, twi_ref, wch_ref, a_ref, o_ref):
    w2r = w2r_ref[...]
    w2i = w2i_ref[...]
    wch = wch_ref[...].astype(jnp.bfloat16)
    def position_dft(j):
        tr = twr_ref[j:j + 1, :]
        ti = twi_ref[j:j + 1, :]
        gr = (w2r * tr - w2i * ti).astype(jnp.bfloat16)
        gi = (w2r * ti + w2i * tr).astype(jnp.bfloat16)
        g4 = jnp.concatenate(
            [jnp.concatenate([gr, -gi], axis=1), jnp.concatenate([gi, gr], axis=1)], axis=0)
        a = jnp.concatenate([a_ref[0, j], a_ref[1, j]], axis=0)
        return jnp.dot(g4, a, preferred_element_type=jnp.float32).astype(jnp.bfloat16)

    def channel_dft(y):
        n2 = y.shape[0] // 2
        yr, yi = y[:n2], y[n2:]
        groups = []
        for g in range(F_GROUPS):
            sl = slice(g * F_GROUP_CH, (g + 1) * F_GROUP_CH)
            lhs = jnp.concatenate([yr[:, sl], yi[:, sl]], axis=1)
            groups.append(jnp.dot(lhs, wch, preferred_element_type=jnp.float32).astype(o_ref.dtype))
        return jnp.concatenate(groups, axis=1)

    slabs = []
    y_next = position_dft(0)
    for j in range(DFT_SLABS):
        y = y_next
        if j + 1 < DFT_SLABS:
            y_next = position_dft(j + 1)
        slabs.append(channel_dft(y))
    o_ref[...] = jnp.swapaxes(jnp.stack(slabs, axis=0), 0, 1)


def _dft_pos_c(a, w2r, w2i, twr, twi, wch):
    _, n1, n2, c = a.shape
    full = lambda shape: pl.BlockSpec(shape, lambda i: (0,) * len(shape))
    return pl.pallas_call(
        _dft_pos_c_kernel,
        grid=(n1 // DFT_SLABS,),
        in_specs=[
            full((n2, n2)), full((n2, n2)),
            pl.BlockSpec((DFT_SLABS, n2), lambda i: (i, 0)),
            pl.BlockSpec((DFT_SLABS, n2), lambda i: (i, 0)),
            full((2 * F_GROUP_CH, F_GROUP_CH)),
            pl.BlockSpec((2, DFT_SLABS, n2, c), lambda i: (0, i, 0, 0)),
        ],
        out_specs=pl.BlockSpec((n2, DFT_SLABS, c), lambda i: (0, i, 0)),
        out_shape=jax.ShapeDtypeStruct((n2, n1, c), jnp.bfloat16),
        compiler_params=_cparams("parallel"),
        name="dft_pos_c",
    )(w2r, w2i, twr, twi, wch, a)


def _fourier_mix(z, tables):
    s = z.shape[0]
    n2 = DFT_N2
    n1 = s // n2
    fa, w2r, w2i, twr, twi, wch = tables
    a = _dft_pos_a(z.reshape(n1, n2, z.shape[1]), fa)
    y = _dft_pos_c(a, w2r, w2i, twr, twi, wch)
    return y.reshape(s, F_WIDTH)


NA_RPB_H = 2 * WIN_H - 1
NA_RPB_W = 2 * WIN_W - 1
NA_KROW_PAIRS = NA_KROWS // 2


def _natten_row_windows():
    i = np.arange(NA_QROWS)
    mid = np.full(NA_QROWS, -(WIN_H // 2))
    first = np.where(i < WIN_H // 2, -i, mid)
    last = np.where(i > WIN_H // 2, -i, mid)
    return (first, mid, last)


def _natten_bias_kernel(rpb_ref, w_ref, o_ref, wb_ref, cb_ref):
    h = pl.program_id(0)
    wb_ref[...] = w_ref[...].astype(wb_ref.dtype)
    shape = (GRID_W, 2 * GRID_W)
    qc = lax.broadcasted_iota(jnp.int32, shape, 0)
    lane = lax.broadcasted_iota(jnp.int32, shape, 1)
    kc = lane & (GRID_W - 1)
    col_start = jnp.clip(qc - WIN_W // 2, 0, GRID_W - WIN_W)
    col_ok = (kc >= col_start) & (kc < col_start + WIN_W)
    dc = jnp.where(col_ok, kc - qc + WIN_W - 1, -1)
    neg = jnp.full(shape, NEG_BIAS, jnp.float32)
    for d in range(NA_RPB_H):
        tile = neg
        for t in range(NA_RPB_W):
            r = rpb_ref[(h * NA_RPB_H + d) * NA_RPB_W + t] * (NA_HEAD_DIM ** 0.5)
            tile = jnp.where(dc == t, r, tile)
        cb_ref[d] = tile
    left = lane < GRID_W
    for v, lo in enumerate(_natten_row_windows()):
        for i in range(NA_QROWS):
            for jj in range(NA_KROW_PAIRS):
                dr = (2 * jj - NA_KROW_OFFSET - i, 2 * jj + 1 - NA_KROW_OFFSET - i)
                ok = [lo[i] <= d <= lo[i] + WIN_H - 1 for d in dr]
                halves = [cb_ref[d + WIN_H - 1] if good else neg for d, good in zip(dr, ok)]
                tile = jnp.where(left, halves[0], halves[1]) if any(ok) else neg
                o_ref[v, 0, i * GRID_W:(i + 1) * GRID_W, jj * 2 * GRID_W:(jj + 1) * 2 * GRID_W] = tile


def _natten_bias(rpb, w_in):
    blk = NA_QROWS * GRID_W
    keys = NA_KROWS * GRID_W
    rows = _cast_rows_per_step(w_in.shape[0], NA_HEADS)
    assert w_in.shape[0] == rows * NA_HEADS
    wspec = pl.BlockSpec((rows, w_in.shape[1]), lambda h: (h, 0))
    return pl.pallas_call(
        _natten_bias_kernel,
        grid=(NA_HEADS,),
        in_specs=[pl.BlockSpec(memory_space=pltpu.SMEM), wspec],
        out_specs=[pl.BlockSpec((3, 1, blk, keys), lambda h: (0, h, 0, 0)), wspec],
        out_shape=[jax.ShapeDtypeStruct((3, NA_HEADS, blk, keys), jnp.float32),
                   jax.ShapeDtypeStruct(w_in.shape, jnp.bfloat16)],
        scratch_shapes=[pltpu.VMEM((NA_RPB_H, GRID_W, 2 * GRID_W), jnp.float32)],
        compiler_params=_cparams("parallel"),
        name="natten_bias",
    )(rpb.astype(jnp.float32).reshape(-1), w_in)


NA_HEADS_PER_STEP = 4


NA_ROW_BLOCK = 32


def _natten_kernel(q_ref, kp_ref, kc_ref, kn_ref, vp_ref, vc_ref, vn_ref, b_ref, o_ref, t_ref, p_ref):
    half = (NA_KROWS - NA_QROWS) // 2 * GRID_W
    blk = NA_QROWS * GRID_W
    nq = blk // 2
    nkeys = blk + half
    exp2_scale = (NA_HEAD_DIM ** -0.5) * math.log2(math.e)
    ones = jnp.ones((nkeys, NA_HEAD_DIM), jnp.bfloat16)
    chains = []
    for h in range(NA_HEADS_PER_STEP):
        sl = slice(h * NA_HEAD_DIM, (h + 1) * NA_HEAD_DIM)
        chains.append((h, sl, 0, 0, (kp_ref, kc_ref), (vp_ref, vc_ref)))
        chains.append((h, sl, nq, half, (kc_ref, kn_ref), (vc_ref, vn_ref)))

    def scores(ch):
        h, sl, row0, key0, k_refs, v_refs = chains[ch]
        kwin = jnp.concatenate([r[:, sl] for r in k_refs], axis=0)
        t_ref[ch] = lax.dot_general(q_ref[row0:row0 + nq, sl], kwin, (((1,), (1,)), ((), ())),
                                    preferred_element_type=jnp.float32)

    def softmax(ch):
        h, sl, row0, key0, k_refs, v_refs = chains[ch]
        biased = lambda r0: (t_ref[ch, r0:r0 + NA_ROW_BLOCK, :]
                             + b_ref[0, h, row0 + r0:row0 + r0 + NA_ROW_BLOCK, key0:key0 + nkeys])
        row_blocks = range(0, nq, NA_ROW_BLOCK)
        maxes = [jnp.max(biased(r0), axis=-1, keepdims=True) for r0 in row_blocks]
        for m, r0 in zip(maxes, row_blocks):
            p_ref[ch, r0:r0 + NA_ROW_BLOCK, :] = jnp.exp2((biased(r0) - m) * exp2_scale).astype(p_ref.dtype)

    def weighted_values(ch):
        h, sl, row0, key0, k_refs, v_refs = chains[ch]
        vwin = jnp.concatenate([r[:, sl] for r in v_refs], axis=0)
        ol = jnp.dot(p_ref[ch], jnp.concatenate([vwin, ones], axis=1), preferred_element_type=jnp.float32)
        o_ref[row0:row0 + nq, sl] = (ol[:, :NA_HEAD_DIM] / ol[:, NA_HEAD_DIM:]).astype(o_ref.dtype)

    scores(0)
    for ch in range(len(chains)):
        if ch + 1 < len(chains):
            scores(ch + 1)
        softmax(ch)
        weighted_values(ch)


def _natten(z, bias):
    s = z.shape[0]
    blk = NA_QROWS * GRID_W
    nb = s // blk
    width = NA_HEADS_PER_STEP * NA_HEAD_DIM
    q0 = F_WIDTH // width
    k0 = q0 + NA_WIDTH // width
    v0 = k0 + NA_WIDTH // width
    half = (NA_KROWS - NA_QROWS) // 2 * GRID_W
    assert blk == 2 * half
    prev = lambda b: jnp.maximum(2 * b - 1, 0)
    nxt = lambda b: jnp.minimum(2 * b + 2, 2 * nb - 1)
    variant = lambda b: jnp.where(b == 0, 0, jnp.where(b == nb - 1, 2, 1))
    spec = lambda col0, rowfn, rows=blk: pl.BlockSpec((rows, width), lambda h, b: (rowfn(b), col0 + h))
    same = lambda b: b
    return pl.pallas_call(
        _natten_kernel,
        grid=(NA_HEADS // NA_HEADS_PER_STEP, nb),
        in_specs=[
            spec(q0, same),
            spec(k0, prev, half), spec(k0, same), spec(k0, nxt, half),
            spec(v0, prev, half), spec(v0, same), spec(v0, nxt, half),
            pl.BlockSpec((1, NA_HEADS_PER_STEP, blk, NA_KROWS * GRID_W), lambda h, b: (variant(b), h, 0, 0)),
        ],
        out_specs=pl.BlockSpec((blk, width), lambda h, b: (b, h)),
        out_shape=jax.ShapeDtypeStruct((s, NA_WIDTH), jnp.bfloat16),
        scratch_shapes=[pltpu.VMEM((2 * NA_HEADS_PER_STEP, blk // 2, blk + blk // 2), jnp.float32),
                        pltpu.VMEM((2 * NA_HEADS_PER_STEP, blk // 2, blk + blk // 2), jnp.bfloat16)],
        compiler_params=_cparams("parallel", "arbitrary"),
        name="natten",
    )(z, z, z, z, z, z, z, bias)


def _layer_norm(h, g, b):
    mu = jnp.mean(h, axis=-1, keepdims=True)
    d = h - mu
    var = jnp.mean(d * d, axis=-1, keepdims=True)
    return d * lax.rsqrt(var + LN_EPS) * g + b


def _mix_ln1_kernel(x_ref, fm_ref, at_ref, ga_ref, gb_ref, p_ref, fw_ref, nw_ref, wo_ref, pg_ref, pp_ref,
                    g_ref, b_ref, r_ref, x1b_ref, h_ref):
    @pl.when(pl.program_id(0) == 0)
    def _():
        h_ref[...] = jnp.zeros_like(h_ref)

    a = jnp.dot(fm_ref[...], fw_ref[...], preferred_element_type=jnp.float32)
    b = jnp.dot(at_ref[...], nw_ref[...], preferred_element_type=jnp.float32)
    emb = jnp.dot(p_ref[...].astype(jnp.bfloat16), pp_ref[...], preferred_element_type=jnp.float32)

    x1 = _layer_norm(h_ref[...], g_ref[...], b_ref[...])
    x1b = x1.astype(jnp.bfloat16)
    gate = jax.nn.sigmoid(jnp.dot(x1b, pg_ref[...], preferred_element_type=jnp.float32))
    r_ref[...] = DN_ALPHA * x1 + gate * emb
    x1b_ref[...] = x1b

    merged = ga_ref[...].astype(jnp.float32) * a + gb_ref[...].astype(jnp.float32) * b
    h_ref[...] = DN_ALPHA * x_ref[...] + jnp.dot(merged.astype(jnp.bfloat16), wo_ref[...],
                                                 preferred_element_type=jnp.float32)


def _resident(shape):
    return pl.BlockSpec(shape, lambda *_: (0,) * len(shape), pipeline_mode=pl.Buffered(1))


def _mix_ln1(x, fmix, att, gates, p, fw_b, nw_b, wo_b, pg_b, pp_b, ln_g, ln_b, tm=256):
    s = x.shape[0]
    n = s // tm
    cur = lambda width, col=0: pl.BlockSpec((tm, width), lambda i: (jnp.minimum(i, n - 1), col))
    lag = lambda width: pl.BlockSpec((tm, width), lambda i: (jnp.maximum(i - 1, 0), 0))
    return pl.pallas_call(
        _mix_ln1_kernel,
        grid=(n + 1,),
        in_specs=[
            cur(D_MODEL), cur(F_WIDTH), cur(NA_WIDTH), cur(D_MODEL, 0), cur(D_MODEL, 1), lag(PLE_DIM),
            _resident((F_WIDTH, D_MODEL)), _resident((NA_WIDTH, D_MODEL)), _resident((D_MODEL, D_MODEL)),
            _resident((D_MODEL, D_MODEL)), _resident((PLE_DIM, D_MODEL)),
            _resident((1, D_MODEL)), _resident((1, D_MODEL)),
        ],
        out_specs=[lag(D_MODEL), lag(D_MODEL)],
        out_shape=[jax.ShapeDtypeStruct((s, D_MODEL), jnp.float32),
                   jax.ShapeDtypeStruct((s, D_MODEL), jnp.bfloat16)],
        scratch_shapes=[pltpu.VMEM((tm, D_MODEL), jnp.float32)],
        compiler_params=_cparams("arbitrary"),
        name="mix_ln1",
    )(x, fmix, att, gates, gates, p, fw_b, nw_b, wo_b, pg_b, pp_b, ln_g, ln_b)


FFN_HALO = 16


def _gelu_tanh(u):
    k = math.sqrt(2.0 / math.pi)
    return 0.5 * u * (1.0 + jnp.tanh(u * (k + (k * 0.044715) * (u * u))))


LANES = 128
FFN_ROW_BLOCK = 128
FFN_STAGE_SLOTS = 3


def _ffn_up_kernel(xp_ref, xc_ref, xn_ref, wa_ref, wb_ref, ca_ref, cb_ref, ba_ref, bb_ref, h_ref,
                   xe_ref, ua_ref, ub_ref, *, tm, tn, n_row_tiles):
    i = pl.program_id(1)
    xe_ref[:FFN_HALO, :] = jnp.where(i > 0, xp_ref[...], jnp.zeros_like(xp_ref))
    xe_ref[FFN_HALO:FFN_HALO + tm, :] = xc_ref[...]
    xe_ref[FFN_HALO + tm:, :] = jnp.where(i < n_row_tiles - 1, xn_ref[...], jnp.zeros_like(xn_ref))
    lanes_per_chunk = MXU_COLS // LANES

    def matmul_chunk(c, slot):
        sl = slice(c * MXU_COLS, (c + 1) * MXU_COLS)
        for w_ref, u_ref in ((wa_ref, ua_ref), (wb_ref, ub_ref)):
            u = jnp.dot(xe_ref[...], w_ref[:, sl], preferred_element_type=jnp.float32)
            for k in range(lanes_per_chunk):
                u_ref[slot, k] = u[:, k * LANES:(k + 1) * LANES]

    def epilogue_chunk(c, slot):
        for k in range(lanes_per_chunk):
            sl = slice(c * MXU_COLS + k * LANES, c * MXU_COLS + (k + 1) * LANES)
            taps = []
            for c_ref, b_ref in ((ca_ref, ba_ref), (cb_ref, bb_ref)):
                taps.append([jnp.broadcast_to(c_ref[t:t + 1, sl], (FFN_ROW_BLOCK, LANES)) for t in range(3)]
                            + [jnp.broadcast_to(b_ref[:, sl], (FFN_ROW_BLOCK, LANES))])
            for r in range(0, tm, FFN_ROW_BLOCK):
                halves = []
                for u_ref, (c0, c1, c2, bias) in zip((ua_ref, ub_ref), taps):
                    lo = u_ref[slot, k, pl.ds(FFN_HALO - 1 + r, FFN_ROW_BLOCK), :]
                    mid = u_ref[slot, k, pl.ds(FFN_HALO + r, FFN_ROW_BLOCK), :]
                    hi = u_ref[slot, k, pl.ds(FFN_HALO + 1 + r, FFN_ROW_BLOCK), :]
                    halves.append(lo * c0 + mid * c1 + hi * c2 + bias)
                h_ref[pl.ds(r, FFN_ROW_BLOCK), sl] = (_gelu_tanh(halves[0]) * halves[1]).astype(h_ref.dtype)

    n_chunks = tn // MXU_COLS
    for c in range(n_chunks):
        matmul_chunk(c, c % FFN_STAGE_SLOTS)
        if c > 0:
            epilogue_chunk(c - 1, (c - 1) % FFN_STAGE_SLOTS)
    epilogue_chunk(n_chunks - 1, (n_chunks - 1) % FFN_STAGE_SLOTS)


def _ffn_up(x1b, up_b, conv, conv_b, tm=1024, tn=D_FF // 2):
    s = x1b.shape[0]
    n_row_tiles = s // tm
    n_col = D_FF // tn
    halo_per_tile = tm // FFN_HALO
    n_halo_blocks = s // FFN_HALO
    prev = lambda j, i: (jnp.maximum(i * halo_per_tile - 1, 0), 0)
    nxt = lambda j, i: (jnp.minimum((i + 1) * halo_per_tile, n_halo_blocks - 1), 0)
    wspec = lambda shape, off: pl.BlockSpec(shape, lambda j, i: (0, j + off), pipeline_mode=pl.Buffered(1))
    return pl.pallas_call(
        functools.partial(_ffn_up_kernel, tm=tm, tn=tn, n_row_tiles=n_row_tiles),
        grid=(n_col, n_row_tiles),
        in_specs=[
            pl.BlockSpec((FFN_HALO, D_MODEL), prev),
            pl.BlockSpec((tm, D_MODEL), lambda j, i: (i, 0)),
            pl.BlockSpec((FFN_HALO, D_MODEL), nxt),
            wspec((D_MODEL, tn), 0), wspec((D_MODEL, tn), n_col),
            wspec((3, tn), 0), wspec((3, tn), n_col),
            wspec((1, tn), 0), wspec((1, tn), n_col),
        ],
        out_specs=pl.BlockSpec((tm, tn), lambda j, i: (i, j)),
        out_shape=jax.ShapeDtypeStruct((s, D_FF), jnp.bfloat16),
        scratch_shapes=[pltpu.VMEM((tm + 2 * FFN_HALO, D_MODEL), jnp.bfloat16),
                        pltpu.VMEM((FFN_STAGE_SLOTS, MXU_COLS // LANES, tm + 2 * FFN_HALO, LANES), jnp.float32),
                        pltpu.VMEM((FFN_STAGE_SLOTS, MXU_COLS // LANES, tm + 2 * FFN_HALO, LANES), jnp.float32)],
        compiler_params=_cparams("arbitrary", "arbitrary"),
        name="ffn_up",
    )(x1b, x1b, x1b, up_b, up_b, conv, conv, conv_b, conv_b)


SUBLANES = 8


def _zero_after(x):
    rows, cols = x.shape
    bits = pltpu.bitcast(x, jnp.uint32)
    acc = bits[:SUBLANES]
    for r in range(SUBLANES, rows, SUBLANES):
        acc = acc | bits[r:r + SUBLANES]
    tile = acc[:, :LANES]
    for c in range(LANES, cols, LANES):
        tile = tile | acc[:, c:c + LANES]
    zeros = pltpu.bitcast((tile >> 16) >> 16, jnp.float32)
    column = jnp.concatenate([zeros] * (rows // SUBLANES), axis=0)
    return jnp.concatenate([column] * (cols // LANES), axis=1)


def _ffn_down_ln2_kernel(r_ref, h_ref, wd_ref, g_ref, b_ref, o_ref, acc_ref, sum_ref, *, n_row_tiles):
    i = pl.program_id(0)
    k = pl.program_id(1)
    partial = lambda: jnp.dot(h_ref[...], wd_ref[...], preferred_element_type=jnp.float32)
    norm_previous = lambda: _layer_norm(sum_ref[...], g_ref[...], b_ref[...])

    @pl.when((i == 0) & (k == 0))
    def _():
        sum_ref[...] = jnp.zeros_like(sum_ref)

    @pl.when((k == 0) & (i < n_row_tiles))
    def _():
        y = norm_previous()
        o_ref[...] = y
        acc_ref[...] = (r_ref[...] + _zero_after(y)) + partial()

    @pl.when((k == 0) & (i == n_row_tiles))
    def _():
        o_ref[...] = norm_previous()

    @pl.when((k == 1) & (i < n_row_tiles))
    def _():
        sum_ref[...] = acc_ref[...] + partial()


def _ffn_down_ln2(r, h, wd_b, ln_g, ln_b, tm=512):
    s = r.shape[0]
    n = s // tm
    tk = D_FF // 2
    assert tk % MXU_COLS == 0
    cur = lambda i: jnp.minimum(i, n - 1)
    return pl.pallas_call(
        functools.partial(_ffn_down_ln2_kernel, n_row_tiles=n),
        grid=(n + 1, 2),
        in_specs=[
            pl.BlockSpec((tm, D_MODEL), lambda i, k: (cur(i), 0)),
            pl.BlockSpec((tm, tk), lambda i, k: (cur(i), k)),
            pl.BlockSpec((tk, D_MODEL), lambda i, k: (k, 0)),
            _resident((1, D_MODEL)), _resident((1, D_MODEL)),
        ],
        out_specs=pl.BlockSpec((tm, D_MODEL), lambda i, k: (jnp.maximum(i - 1, 0), 0)),
        out_shape=jax.ShapeDtypeStruct((s, D_MODEL), jnp.float32),
        scratch_shapes=[pltpu.VMEM((tm, D_MODEL), jnp.float32), pltpu.VMEM((tm, D_MODEL), jnp.float32)],
        compiler_params=_cparams("arbitrary", "arbitrary"),
        name="ffn_down_ln2",
    )(r, h, wd_b, ln_g, ln_b)


CAST_IN_KERNEL = ("fourier_w", "natten_w", "w_out", "ple_gate", "ple_proj", "ffn_up", "ffn_down")


def _encoder_layer(x, p, wts, bias, tables):
    pending = [name for name in CAST_IN_KERNEL if wts[name].dtype != jnp.bfloat16]
    z, gates, cast = _in_proj(x, wts["w_in"], wts["gate_b"], [wts[name] for name in pending])
    wts = {**wts, **dict(zip(pending, cast))}
    return _layer_after_in_proj(x, z, gates, p, wts, bias, tables), wts


def _layer_after_in_proj(x, z, gates, p, wts, bias, tables):
    fmix = _fourier_mix(z, tables)
    att = _natten(z, bias)
    r, x1b = _mix_ln1(x, fmix, att, gates, p, wts["fourier_w"], wts["natten_w"], wts["w_out"],
                      wts["ple_gate"], wts["ple_proj"], wts["ln1_g"], wts["ln1_b"])
    h = _ffn_up(x1b, wts["ffn_up"], wts["ffn_conv"], wts["ffn_conv_b"])
    return _ffn_down_ln2(r, h, wts["ffn_down"], wts["ln2_g"], wts["ln2_b"])


def kernel(x_prompt, x_sample, p_prompt, p_sample, w_in, gate_b, fourier_w, natten_rpb, natten_w, w_out,
           ln1_g, ln1_b, ffn_up, ffn_conv, ffn_conv_b, ffn_down, ple_proj, ple_gate, ln2_g, ln2_b):
    assert w_in.shape[0] == DEPTH
    xp, xs = x_prompt[0], x_sample[0]
    row = lambda v: v.reshape(1, -1)
    for i in range(DEPTH):
        wts = {
            "gate_b": row(gate_b[i]),
            "fourier_w": fourier_w[i], "natten_w": natten_w[i], "w_out": w_out[i],
            "ln1_g": row(ln1_g[i]), "ln1_b": row(ln1_b[i]),
            "ffn_up": ffn_up[i], "ffn_conv": ffn_conv[i], "ffn_conv_b": row(ffn_conv_b[i]),
            "ffn_down": ffn_down[i], "ple_proj": ple_proj[i], "ple_gate": ple_gate[i],
            "ln2_g": row(ln2_g[i]), "ln2_b": row(ln2_b[i]),
        }
        bias, wts["w_in"] = _natten_bias(natten_rpb[i], w_in[i])
        xp, wts = _encoder_layer(xp, p_prompt[i, 0], wts, bias, _dft_tables(xp.shape[0]))
        xs, wts = _encoder_layer(xs, p_sample[i, 0], wts, bias, _dft_tables(xs.shape[0]))
    return (xp[None], xs[None])
```

```python
import functools
import math

import jax
import jax.numpy as jnp
import numpy as np
from jax import lax
from jax.experimental import pallas as pl
from jax.experimental.pallas import tpu as pltpu

D_MODEL = 2048
GRID_W = 64
PLE_DIM = 256
F_WIDTH = 1024
F_GROUPS = 8
F_GROUP_CH = F_WIDTH // F_GROUPS
NA_HEADS = 8
NA_HEAD_DIM = 128
NA_WIDTH = NA_HEADS * NA_HEAD_DIM
WIN_H = 8
WIN_W = 16
D_FF = 5632
IN_COLS = F_WIDTH + 3 * NA_WIDTH + 2 * D_MODEL
LN_EPS = 1e-5
DEPTH = 1
DN_ALPHA = (2.0 * DEPTH) ** 0.25

VMEM_PHYSICAL_BYTES = 64 * 1024 * 1024
VMEM_LIMIT_BYTES = 56 * 1024 * 1024
VMEM_INTERNAL_SCRATCH_BYTES = 6 * 1024 * 1024
NEG_BIAS = -1e30

DFT_N2 = 128
NA_QROWS = 8
NA_KROWS = 16
NA_KROW_OFFSET = 4


def _cparams(*semantics, vmem_limit_bytes=VMEM_LIMIT_BYTES):
    return pltpu.CompilerParams(dimension_semantics=semantics, vmem_limit_bytes=vmem_limit_bytes)


MXU_COLS = 256


CAST_ROW_ALIGN = 16


def _cast_rows_per_step(rows, n_steps):
    per_step = -(-rows // n_steps)
    per_step = -(-per_step // CAST_ROW_ALIGN) * CAST_ROW_ALIGN
    while rows % per_step:
        per_step += CAST_ROW_ALIGN
    return per_step


def _in_proj_kernel(*refs, tn, n_cast):
    x_ref, wz_ref, wg_ref, gb_ref = refs[:4]
    cast_in = refs[4:4 + n_cast]
    z_ref, g_ref = refs[4 + n_cast:6 + n_cast]
    cast_out = refs[6 + n_cast:6 + 2 * n_cast]
    xb_ref = refs[-1]

    @pl.when(pl.program_id(1) == 0)
    def _():
        xb_ref[...] = x_ref[...].astype(jnp.bfloat16)

    for src_ref, dst_ref in zip(cast_in, cast_out):
        dst_ref[...] = src_ref[...].astype(dst_ref.dtype)

    xb = xb_ref[...]
    for c in range(0, tn, MXU_COLS):
        sl = slice(c, c + MXU_COLS)
        acc = jnp.dot(xb, wg_ref[:, sl], preferred_element_type=jnp.float32)
        g_ref[:, sl] = jax.nn.sigmoid(acc + gb_ref[:, sl]).astype(g_ref.dtype)
    for c in range(0, tn, MXU_COLS):
        sl = slice(c, c + MXU_COLS)
        z_ref[:, sl] = jnp.dot(xb, wz_ref[:, sl], preferred_element_type=jnp.float32).astype(z_ref.dtype)


def _in_proj(x, w_in_b, gate_b, cast_weights=(), tm=1024, tn=1024):
    s = x.shape[0]
    z_cols = F_WIDTH + 3 * NA_WIDTH
    g_cols = IN_COLS - z_cols
    assert z_cols == g_cols
    n_col = z_cols // tn
    n_steps = (s // tm) * n_col
    cast_rows = [_cast_rows_per_step(w.shape[0], n_steps) for w in cast_weights]
    cast_blocks = tuple(w.shape[0] // r for w, r in zip(cast_weights, cast_rows))
    vmem_bytes = (2 * tm * D_MODEL * 4 + 2 * 2 * D_MODEL * tn * 2 + 2 * 2 * tm * tn * 2 + tm * D_MODEL * 2
                  + sum(2 * r * w.shape[1] * (4 + 2) for w, r in zip(cast_weights, cast_rows))
                  + VMEM_INTERNAL_SCRATCH_BYTES)
    assert vmem_bytes <= VMEM_PHYSICAL_BYTES - VMEM_INTERNAL_SCRATCH_BYTES // 2, vmem_bytes
    cast_specs = [
        pl.BlockSpec((r, w.shape[1]), lambda i, j, nb=nb: (jnp.minimum(i * n_col + j, nb - 1), 0))
        for w, r, nb in zip(cast_weights, cast_rows, cast_blocks)]
    outs = pl.pallas_call(
        functools.partial(_in_proj_kernel, tn=tn, n_cast=len(cast_weights)),
        grid=(s // tm, n_col),
        in_specs=[
            pl.BlockSpec((tm, D_MODEL), lambda i, j: (i, 0)),
            pl.BlockSpec((D_MODEL, tn), lambda i, j: (0, j)),
            pl.BlockSpec((D_MODEL, tn), lambda i, j: (0, j + n_col)),
            pl.BlockSpec((1, tn), lambda i, j: (0, j)),
        ] + cast_specs,
        out_specs=[pl.BlockSpec((tm, tn), lambda i, j: (i, j)),
                   pl.BlockSpec((tm, tn), lambda i, j: (i, j))] + cast_specs,
        out_shape=[jax.ShapeDtypeStruct((s, z_cols), jnp.bfloat16),
                   jax.ShapeDtypeStruct((s, g_cols), jnp.bfloat16)]
        + [jax.ShapeDtypeStruct(w.shape, jnp.bfloat16) for w in cast_weights],
        scratch_shapes=[pltpu.VMEM((tm, D_MODEL), jnp.bfloat16)],
        compiler_params=_cparams("arbitrary", "arbitrary", vmem_limit_bytes=vmem_bytes),
        name="in_proj",
    )(x, w_in_b, w_in_b, gate_b, *cast_weights)
    return outs[0], outs[1], tuple(outs[2:])


def _dft_tables(s):
    n2 = DFT_N2
    n1 = s // n2
    k1 = np.arange(n1)
    w1 = np.exp(-2j * np.pi * np.outer(k1, k1) / n1) / math.sqrt(n1)
    fa = np.concatenate([w1.real, w1.imag], axis=0)
    k2 = np.arange(n2)
    w2 = np.exp(-2j * np.pi * np.outer(k2, k2) / n2) / math.sqrt(n2)
    tw = np.exp(-2j * np.pi * np.outer(k1, k2) / s)
    c = np.arange(F_GROUP_CH)
    wc = np.exp(-2j * np.pi * np.outer(c, c) / F_GROUP_CH) / math.sqrt(F_GROUP_CH)
    wch = np.concatenate([wc.real, -wc.imag], axis=0)
    f32 = lambda a: jnp.asarray(a, dtype=jnp.float32)
    return (f32(fa), f32(w2.real), f32(w2.imag), f32(tw.real), f32(tw.imag), f32(wch))


DFT_SLABS = 16


def _dft_pos_a_kernel(fa_ref, x_ref, o_ref, *, n1):
    fa = fa_ref[...].astype(jnp.bfloat16)
    xt = jnp.swapaxes(x_ref[...], 0, 1)
    re, im = [], []
    for j in range(DFT_SLABS):
        p = jnp.dot(fa, xt[j], preferred_element_type=jnp.float32)
        re.append(p[:n1].astype(o_ref.dtype))
        im.append(p[n1:].astype(o_ref.dtype))
    o_ref[0] = jnp.swapaxes(jnp.stack(re, axis=0), 0, 1)
    o_ref[1] = jnp.swapaxes(jnp.stack(im, axis=0), 0, 1)


def _dft_pos_a(z3, fa):
    n1, n2, _ = z3.shape
    return pl.pallas_call(
        functools.partial(_dft_pos_a_kernel, n1=n1),
        grid=(n2 // DFT_SLABS,),
        in_specs=[
            pl.BlockSpec((2 * n1, n1), lambda i: (0, 0)),
            pl.BlockSpec((n1, DFT_SLABS, F_WIDTH), lambda i: (0, i, 0)),
        ],
        out_specs=pl.BlockSpec((2, n1, DFT_SLABS, F_WIDTH), lambda i: (0, 0, i, 0)),
        out_shape=jax.ShapeDtypeStruct((2, n1, n2, F_WIDTH), jnp.bfloat16),
        compiler_params=_cparams("parallel"),
        name="dft_pos_a",
    )(fa, z3)


def _dft_pos_c_kernel(w2r_ref, w2i_ref, twr_ref, twi_ref, wch_ref, a_ref, o_ref):
    w2r = w2r_ref[...]
    w2i = w2i_ref[...]
    wch = wch_ref[...].astype(jnp.bfloat16)
    def position_dft(j):
        tr = twr_ref[j:j + 1, :]
        ti = twi_ref[j:j + 1, :]
        gr = (w2r * tr - w2i * ti).astype(jnp.bfloat16)
        gi = (w2r * ti + w2i * tr).astype(jnp.bfloat16)
        g4 = jnp.concatenate(
            [jnp.concatenate([gr, -gi], axis=1), jnp.concatenate([gi, gr], axis=1)], axis=0)
        a = jnp.concatenate([a_ref[0, j], a_ref[1, j]], axis=0)
        return jnp.dot(g4, a, preferred_element_type=jnp.float32).astype(jnp.bfloat16)

    def channel_dft(ys):
        n2 = ys[0].shape[0] // 2
        groups = []
        for g in range(F_GROUPS):
            sl = slice(g * F_GROUP_CH, (g + 1) * F_GROUP_CH)
            lhs = jnp.concatenate(
                [jnp.concatenate([y[:n2, sl], y[n2:, sl]], axis=1) for y in ys], axis=0)
            groups.append(jnp.dot(lhs, wch, preferred_element_type=jnp.float32).astype(o_ref.dtype))
        full = jnp.concatenate(groups, axis=1)
        return full.reshape(len(ys), n2, full.shape[1])

    ys = [position_dft(j) for j in range(DFT_SLABS)]
    o_ref[...] = jnp.swapaxes(channel_dft(ys), 0, 1)


def _dft_pos_c(a, w2r, w2i, twr, twi, wch):
    _, n1, n2, c = a.shape
    full = lambda shape: pl.BlockSpec(shape, lambda i: (0,) * len(shape))
    return pl.pallas_call(
        _dft_pos_c_kernel,
        grid=(n1 // DFT_SLABS,),
        in_specs=[
            full((n2, n2)), full((n2, n2)),
            pl.BlockSpec((DFT_SLABS, n2), lambda i: (i, 0)),
            pl.BlockSpec((DFT_SLABS, n2), lambda i: (i, 0)),
            full((2 * F_GROUP_CH, F_GROUP_CH)),
            pl.BlockSpec((2, DFT_SLABS, n2, c), lambda i: (0, i, 0, 0)),
        ],
        out_specs=pl.BlockSpec((n2, DFT_SLABS, c), lambda i: (0, i, 0)),
        out_shape=jax.ShapeDtypeStruct((n2, n1, c), jnp.bfloat16),
        compiler_params=_cparams("parallel"),
        name="dft_pos_c",
    )(w2r, w2i, twr, twi, wch, a)


def _fourier_mix(z, tables):
    s = z.shape[0]
    n2 = DFT_N2
    n1 = s // n2
    fa, w2r, w2i, twr, twi, wch = tables
    a = _dft_pos_a(z.reshape(n1, n2, z.shape[1]), fa)
    y = _dft_pos_c(a, w2r, w2i, twr, twi, wch)
    return y.reshape(s, F_WIDTH)


NA_RPB_H = 2 * WIN_H - 1
NA_RPB_W = 2 * WIN_W - 1
NA_KROW_PAIRS = NA_KROWS // 2


def _natten_row_windows():
    i = np.arange(NA_QROWS)
    mid = np.full(NA_QROWS, -(WIN_H // 2))
    first = np.where(i < WIN_H // 2, -i, mid)
    last = np.where(i > WIN_H // 2, -i, mid)
    return (first, mid, last)


def _natten_bias_kernel(rpb_ref, w_ref, o_ref, wb_ref, cb_ref):
    h = pl.program_id(0)
    wb_ref[...] = w_ref[...].astype(wb_ref.dtype)
    shape = (GRID_W, 2 * GRID_W)
    qc = lax.broadcasted_iota(jnp.int32, shape, 0)
    lane = lax.broadcasted_iota(jnp.int32, shape, 1)
    kc = lane & (GRID_W - 1)
    col_start = jnp.clip(qc - WIN_W // 2, 0, GRID_W - WIN_W)
    col_ok = (kc >= col_start) & (kc < col_start + WIN_W)
    dc = jnp.where(col_ok, kc - qc + WIN_W - 1, -1)
    neg = jnp.full(shape, NEG_BIAS, jnp.float32)
    for d in range(NA_RPB_H):
        tile = neg
        for t in range(NA_RPB_W):
            r = rpb_ref[(h * NA_RPB_H + d) * NA_RPB_W + t] * (NA_HEAD_DIM ** 0.5)
            tile = jnp.where(dc == t, r, tile)
        cb_ref[d] = tile
    left = lane < GRID_W
    for v, lo in enumerate(_natten_row_windows()):
        for i in range(NA_QROWS):
            for jj in range(NA_KROW_PAIRS):
                dr = (2 * jj - NA_KROW_OFFSET - i, 2 * jj + 1 - NA_KROW_OFFSET - i)
                ok = [lo[i] <= d <= lo[i] + WIN_H - 1 for d in dr]
                halves = [cb_ref[d + WIN_H - 1] if good else neg for d, good in zip(dr, ok)]
                tile = jnp.where(left, halves[0], halves[1]) if any(ok) else neg
                o_ref[v, 0, i * GRID_W:(i + 1) * GRID_W, jj * 2 * GRID_W:(jj + 1) * 2 * GRID_W] = tile


def _natten_bias(rpb, w_in):
    blk = NA_QROWS * GRID_W
    keys = NA_KROWS * GRID_W
    rows = _cast_rows_per_step(w_in.shape[0], NA_HEADS)
    assert w_in.shape[0] == rows * NA_HEADS
    wspec = pl.BlockSpec((rows, w_in.shape[1]), lambda h: (h, 0))
    return pl.pallas_call(
        _natten_bias_kernel,
        grid=(NA_HEADS,),
        in_specs=[pl.BlockSpec(memory_space=pltpu.SMEM), wspec],
        out_specs=[pl.BlockSpec((3, 1, blk, keys), lambda h: (0, h, 0, 0)), wspec],
        out_shape=[jax.ShapeDtypeStruct((3, NA_HEADS, blk, keys), jnp.float32),
                   jax.ShapeDtypeStruct(w_in.shape, jnp.bfloat16)],
        scratch_shapes=[pltpu.VMEM((NA_RPB_H, GRID_W, 2 * GRID_W), jnp.float32)],
        compiler_params=_cparams("parallel"),
        name="natten_bias",
    )(rpb.astype(jnp.float32).reshape(-1), w_in)


NA_HEADS_PER_STEP = 4


NA_ROW_BLOCK = 32


def _natten_kernel(q_ref, kp_ref, kc_ref, kn_ref, vp_ref, vc_ref, vn_ref, b_ref, o_ref, t_ref, p_ref):
    half = (NA_KROWS - NA_QROWS) // 2 * GRID_W
    blk = NA_QROWS * GRID_W
    nq = blk // 2
    nkeys = blk + half
    exp2_scale = (NA_HEAD_DIM ** -0.5) * math.log2(math.e)
    ones = jnp.ones((nkeys, NA_HEAD_DIM), jnp.bfloat16)
    chains = []
    for h in range(NA_HEADS_PER_STEP):
        sl = slice(h * NA_HEAD_DIM, (h + 1) * NA_HEAD_DIM)
        chains.append((h, sl, 0, 0, (kp_ref, kc_ref), (vp_ref, vc_ref)))
        chains.append((h, sl, nq, half, (kc_ref, kn_ref), (vc_ref, vn_ref)))

    def scores(ch):
        h, sl, row0, key0, k_refs, v_refs = chains[ch]
        kwin = jnp.concatenate([r[:, sl] for r in k_refs], axis=0)
        t_ref[ch] = lax.dot_general(q_ref[row0:row0 + nq, sl], kwin, (((1,), (1,)), ((), ())),
                                    preferred_element_type=jnp.float32)

    def softmax(ch):
        h, sl, row0, key0, k_refs, v_refs = chains[ch]
        biased = lambda r0: (t_ref[ch, r0:r0 + NA_ROW_BLOCK, :]
                             + b_ref[0, h, row0 + r0:row0 + r0 + NA_ROW_BLOCK, key0:key0 + nkeys])
        row_blocks = range(0, nq, NA_ROW_BLOCK)
        maxes = [jnp.max(biased(r0), axis=-1, keepdims=True) for r0 in row_blocks]
        for m, r0 in zip(maxes, row_blocks):
            p_ref[ch, r0:r0 + NA_ROW_BLOCK, :] = jnp.exp2((biased(r0) - m) * exp2_scale).astype(p_ref.dtype)

    def weighted_values(ch):
        h, sl, row0, key0, k_refs, v_refs = chains[ch]
        vwin = jnp.concatenate([r[:, sl] for r in v_refs], axis=0)
        ol = jnp.dot(p_ref[ch], jnp.concatenate([vwin, ones], axis=1), preferred_element_type=jnp.float32)
        o_ref[row0:row0 + nq, sl] = (ol[:, :NA_HEAD_DIM] / ol[:, NA_HEAD_DIM:]).astype(o_ref.dtype)

    scores(0)
    for ch in range(len(chains)):
        if ch + 1 < len(chains):
            scores(ch + 1)
        softmax(ch)
        weighted_values(ch)


def _natten(z, bias):
    s = z.shape[0]
    blk = NA_QROWS * GRID_W
    nb = s // blk
    width = NA_HEADS_PER_STEP * NA_HEAD_DIM
    q0 = F_WIDTH // width
    k0 = q0 + NA_WIDTH // width
    v0 = k0 + NA_WIDTH // width
    half = (NA_KROWS - NA_QROWS) // 2 * GRID_W
    assert blk == 2 * half
    prev = lambda b: jnp.maximum(2 * b - 1, 0)
    nxt = lambda b: jnp.minimum(2 * b + 2, 2 * nb - 1)
    variant = lambda b: jnp.where(b == 0, 0, jnp.where(b == nb - 1, 2, 1))
    spec = lambda col0, rowfn, rows=blk: pl.BlockSpec((rows, width), lambda h, b: (rowfn(b), col0 + h))
    same = lambda b: b
    return pl.pallas_call(
        _natten_kernel,
        grid=(NA_HEADS // NA_HEADS_PER_STEP, nb),
        in_specs=[
            spec(q0, same),
            spec(k0, prev, half), spec(k0, same), spec(k0, nxt, half),
            spec(v0, prev, half), spec(v0, same), spec(v0, nxt, half),
            pl.BlockSpec((1, NA_HEADS_PER_STEP, blk, NA_KROWS * GRID_W), lambda h, b: (variant(b), h, 0, 0)),
        ],
        out_specs=pl.BlockSpec((blk, width), lambda h, b: (b, h)),
        out_shape=jax.ShapeDtypeStruct((s, NA_WIDTH), jnp.bfloat16),
        scratch_shapes=[pltpu.VMEM((2 * NA_HEADS_PER_STEP, blk // 2, blk + blk // 2), jnp.float32),
                        pltpu.VMEM((2 * NA_HEADS_PER_STEP, blk // 2, blk + blk // 2), jnp.bfloat16)],
        compiler_params=_cparams("parallel", "arbitrary"),
        name="natten",
    )(z, z, z, z, z, z, z, bias)


def _layer_norm(h, g, b):
    mu = jnp.mean(h, axis=-1, keepdims=True)
    d = h - mu
    var = jnp.mean(d * d, axis=-1, keepdims=True)
    return d * lax.rsqrt(var + LN_EPS) * g + b


def _mix_ln1_kernel(x_ref, fm_ref, at_ref, ga_ref, gb_ref, p_ref, fw_ref, nw_ref, wo_ref, pg_ref, pp_ref,
                    g_ref, b_ref, r_ref, x1b_ref, h_ref):
    @pl.when(pl.program_id(0) == 0)
    def _():
        h_ref[...] = jnp.zeros_like(h_ref)

    a = jnp.dot(fm_ref[...], fw_ref[...], preferred_element_type=jnp.float32)
    b = jnp.dot(at_ref[...], nw_ref[...], preferred_element_type=jnp.float32)
    emb = jnp.dot(p_ref[...].astype(jnp.bfloat16), pp_ref[...], preferred_element_type=jnp.float32)

    x1 = _layer_norm(h_ref[...], g_ref[...], b_ref[...])
    x1b = x1.astype(jnp.bfloat16)
    gate = jax.nn.sigmoid(jnp.dot(x1b, pg_ref[...], preferred_element_type=jnp.float32))
    r_ref[...] = DN_ALPHA * x1 + gate * emb
    x1b_ref[...] = x1b

    merged = ga_ref[...].astype(jnp.float32) * a + gb_ref[...].astype(jnp.float32) * b
    h_ref[...] = DN_ALPHA * x_ref[...] + jnp.dot(merged.astype(jnp.bfloat16), wo_ref[...],
                                                 preferred_element_type=jnp.float32)


def _resident(shape):
    return pl.BlockSpec(shape, lambda *_: (0,) * len(shape), pipeline_mode=pl.Buffered(1))


def _mix_ln1(x, fmix, att, gates, p, fw_b, nw_b, wo_b, pg_b, pp_b, ln_g, ln_b, tm=256):
    s = x.shape[0]
    n = s // tm
    cur = lambda width, col=0: pl.BlockSpec((tm, width), lambda i: (jnp.minimum(i, n - 1), col))
    lag = lambda width: pl.BlockSpec((tm, width), lambda i: (jnp.maximum(i - 1, 0), 0))
    return pl.pallas_call(
        _mix_ln1_kernel,
        grid=(n + 1,),
        in_specs=[
            cur(D_MODEL), cur(F_WIDTH), cur(NA_WIDTH), cur(D_MODEL, 0), cur(D_MODEL, 1), lag(PLE_DIM),
            _resident((F_WIDTH, D_MODEL)), _resident((NA_WIDTH, D_MODEL)), _resident((D_MODEL, D_MODEL)),
            _resident((D_MODEL, D_MODEL)), _resident((PLE_DIM, D_MODEL)),
            _resident((1, D_MODEL)), _resident((1, D_MODEL)),
        ],
        out_specs=[lag(D_MODEL), lag(D_MODEL)],
        out_shape=[jax.ShapeDtypeStruct((s, D_MODEL), jnp.float32),
                   jax.ShapeDtypeStruct((s, D_MODEL), jnp.bfloat16)],
        scratch_shapes=[pltpu.VMEM((tm, D_MODEL), jnp.float32)],
        compiler_params=_cparams("arbitrary"),
        name="mix_ln1",
    )(x, fmix, att, gates, gates, p, fw_b, nw_b, wo_b, pg_b, pp_b, ln_g, ln_b)


FFN_HALO = 16


def _gelu_tanh(u):
    k = math.sqrt(2.0 / math.pi)
    return 0.5 * u * (1.0 + jnp.tanh(u * (k + (k * 0.044715) * (u * u))))


LANES = 128
FFN_ROW_BLOCK = 128
FFN_STAGE_SLOTS = 4


def _ffn_up_kernel(xp_ref, xc_ref, xn_ref, wa_ref, wb_ref, ca_ref, cb_ref, ba_ref, bb_ref, h_ref,
                   xe_ref, ua_ref, ub_ref, *, tm, tn, n_row_tiles):
    i = pl.program_id(1)
    xe_ref[:FFN_HALO, :] = jnp.where(i > 0, xp_ref[...], jnp.zeros_like(xp_ref))
    xe_ref[FFN_HALO:FFN_HALO + tm, :] = xc_ref[...]
    xe_ref[FFN_HALO + tm:, :] = jnp.where(i < n_row_tiles - 1, xn_ref[...], jnp.zeros_like(xn_ref))
    lanes_per_chunk = MXU_COLS // LANES

    def matmul_chunk(c, slot):
        sl = slice(c * MXU_COLS, (c + 1) * MXU_COLS)
        for w_ref, u_ref in ((wa_ref, ua_ref), (wb_ref, ub_ref)):
            u = jnp.dot(xe_ref[...], w_ref[:, sl], preferred_element_type=jnp.float32)
            for k in range(lanes_per_chunk):
                u_ref[slot, k] = u[:, k * LANES:(k + 1) * LANES]

    def epilogue_chunk(c, slot):
        for k in range(lanes_per_chunk):
            sl = slice(c * MXU_COLS + k * LANES, c * MXU_COLS + (k + 1) * LANES)
            taps = []
            for c_ref, b_ref in ((ca_ref, ba_ref), (cb_ref, bb_ref)):
                taps.append([jnp.broadcast_to(c_ref[t:t + 1, sl], (FFN_ROW_BLOCK, LANES)) for t in range(3)]
                            + [jnp.broadcast_to(b_ref[:, sl], (FFN_ROW_BLOCK, LANES))])
            for r in range(0, tm, FFN_ROW_BLOCK):
                halves = []
                for u_ref, (c0, c1, c2, bias) in zip((ua_ref, ub_ref), taps):
                    lo = u_ref[slot, k, pl.ds(FFN_HALO - 1 + r, FFN_ROW_BLOCK), :]
                    mid = u_ref[slot, k, pl.ds(FFN_HALO + r, FFN_ROW_BLOCK), :]
                    hi = u_ref[slot, k, pl.ds(FFN_HALO + 1 + r, FFN_ROW_BLOCK), :]
                    halves.append(lo * c0 + mid * c1 + hi * c2 + bias)
                h_ref[pl.ds(r, FFN_ROW_BLOCK), sl] = (_gelu_tanh(halves[0]) * halves[1]).astype(h_ref.dtype)

    n_chunks = tn // MXU_COLS
    for c in range(n_chunks):
        matmul_chunk(c, c % FFN_STAGE_SLOTS)
        if c > 0:
            epilogue_chunk(c - 1, (c - 1) % FFN_STAGE_SLOTS)
    epilogue_chunk(n_chunks - 1, (n_chunks - 1) % FFN_STAGE_SLOTS)


def _ffn_up(x1b, up_b, conv, conv_b, tm=1024, tn=D_FF // 2):
    s = x1b.shape[0]
    n_row_tiles = s // tm
    n_col = D_FF // tn
    halo_per_tile = tm // FFN_HALO
    n_halo_blocks = s // FFN_HALO
    prev = lambda j, i: (jnp.maximum(i * halo_per_tile - 1, 0), 0)
    nxt = lambda j, i: (jnp.minimum((i + 1) * halo_per_tile, n_halo_blocks - 1), 0)
    wspec = lambda shape, off: pl.BlockSpec(shape, lambda j, i: (0, j + off), pipeline_mode=pl.Buffered(1))
    return pl.pallas_call(
        functools.partial(_ffn_up_kernel, tm=tm, tn=tn, n_row_tiles=n_row_tiles),
        grid=(n_col, n_row_tiles),
        in_specs=[
            pl.BlockSpec((FFN_HALO, D_MODEL), prev),
            pl.BlockSpec((tm, D_MODEL), lambda j, i: (i, 0)),
            pl.BlockSpec((FFN_HALO, D_MODEL), nxt),
            wspec((D_MODEL, tn), 0), wspec((D_MODEL, tn), n_col),
            wspec((3, tn), 0), wspec((3, tn), n_col),
            wspec((1, tn), 0), wspec((1, tn), n_col),
        ],
        out_specs=pl.BlockSpec((tm, tn), lambda j, i: (i, j)),
        out_shape=jax.ShapeDtypeStruct((s, D_FF), jnp.bfloat16),
        scratch_shapes=[pltpu.VMEM((tm + 2 * FFN_HALO, D_MODEL), jnp.bfloat16),
                        pltpu.VMEM((FFN_STAGE_SLOTS, MXU_COLS // LANES, tm + 2 * FFN_HALO, LANES), jnp.float32),
                        pltpu.VMEM((FFN_STAGE_SLOTS, MXU_COLS // LANES, tm + 2 * FFN_HALO, LANES), jnp.float32)],
        compiler_params=_cparams("arbitrary", "arbitrary"),
        name="ffn_up",
    )(x1b, x1b, x1b, up_b, up_b, conv, conv, conv_b, conv_b)


SUBLANES = 8


def _zero_after(x):
    rows, cols = x.shape
    bits = pltpu.bitcast(x, jnp.uint32)
    acc = bits[:SUBLANES]
    for r in range(SUBLANES, rows, SUBLANES):
        acc = acc | bits[r:r + SUBLANES]
    tile = acc[:, :LANES]
    for c in range(LANES, cols, LANES):
        tile = tile | acc[:, c:c + LANES]
    zeros = pltpu.bitcast((tile >> 16) >> 16, jnp.float32)
    column = jnp.concatenate([zeros] * (rows // SUBLANES), axis=0)
    return jnp.concatenate([column] * (cols // LANES), axis=1)


def _ffn_down_ln2_kernel(r_ref, h_ref, wd_ref, g_ref, b_ref, o_ref, acc_ref, sum_ref, *, n_row_tiles):
    i = pl.program_id(0)
    k = pl.program_id(1)
    partial = lambda: jnp.dot(h_ref[...], wd_ref[...], preferred_element_type=jnp.float32)
    norm_previous = lambda: _layer_norm(sum_ref[...], g_ref[...], b_ref[...])

    @pl.when((i == 0) & (k == 0))
    def _():
        sum_ref[...] = jnp.zeros_like(sum_ref)

    @pl.when((k == 0) & (i < n_row_tiles))
    def _():
        y = norm_previous()
        o_ref[...] = y
        acc_ref[...] = (r_ref[...] + _zero_after(y)) + partial()

    @pl.when((k == 0) & (i == n_row_tiles))
    def _():
        o_ref[...] = norm_previous()

    @pl.when((k == 1) & (i < n_row_tiles))
    def _():
        sum_ref[...] = acc_ref[...] + partial()


def _ffn_down_ln2(r, h, wd_b, ln_g, ln_b, tm=512):
    s = r.shape[0]
    n = s // tm
    tk = D_FF // 2
    assert tk % MXU_COLS == 0
    cur = lambda i: jnp.minimum(i, n - 1)
    return pl.pallas_call(
        functools.partial(_ffn_down_ln2_kernel, n_row_tiles=n),
        grid=(n + 1, 2),
        in_specs=[
            pl.BlockSpec((tm, D_MODEL), lambda i, k: (cur(i), 0)),
            pl.BlockSpec((tm, tk), lambda i, k: (cur(i), k)),
            pl.BlockSpec((tk, D_MODEL), lambda i, k: (k, 0)),
            _resident((1, D_MODEL)), _resident((1, D_MODEL)),
        ],
        out_specs=pl.BlockSpec((tm, D_MODEL), lambda i, k: (jnp.maximum(i - 1, 0), 0)),
        out_shape=jax.ShapeDtypeStruct((s, D_MODEL), jnp.float32),
        scratch_shapes=[pltpu.VMEM((tm, D_MODEL), jnp.float32), pltpu.VMEM((tm, D_MODEL), jnp.float32)],
        compiler_params=_cparams("arbitrary", "arbitrary"),
        name="ffn_down_ln2",
    )(r, h, wd_b, ln_g, ln_b)


CAST_IN_KERNEL = ("fourier_w", "natten_w", "w_out", "ple_gate", "ple_proj", "ffn_up", "ffn_down")


def _encoder_layer(x, p, wts, bias, tables):
    pending = [name for name in CAST_IN_KERNEL if wts[name].dtype != jnp.bfloat16]
    z, gates, cast = _in_proj(x, wts["w_in"], wts["gate_b"], [wts[name] for name in pending])
    wts = {**wts, **dict(zip(pending, cast))}
    return _layer_after_in_proj(x, z, gates, p, wts, bias, tables), wts


def _layer_after_in_proj(x, z, gates, p, wts, bias, tables):
    fmix = _fourier_mix(z, tables)
    att = _natten(z, bias)
    r, x1b = _mix_ln1(x, fmix, att, gates, p, wts["fourier_w"], wts["natten_w"], wts["w_out"],
                      wts["ple_gate"], wts["ple_proj"], wts["ln1_g"], wts["ln1_b"])
    h = _ffn_up(x1b, wts["ffn_up"], wts["ffn_conv"], wts["ffn_conv_b"])
    return _ffn_down_ln2(r, h, wts["ffn_down"], wts["ln2_g"], wts["ln2_b"])


def kernel(x_prompt, x_sample, p_prompt, p_sample, w_in, gate_b, fourier_w, natten_rpb, natten_w, w_out,
           ln1_g, ln1_b, ffn_up, ffn_conv, ffn_conv_b, ffn_down, ple_proj, ple_gate, ln2_g, ln2_b):
    assert w_in.shape[0] == DEPTH
    xp, xs = x_prompt[0], x_sample[0]
    row = lambda v: v.reshape(1, -1)
    for i in range(DEPTH):
        wts = {
            "gate_b": row(gate_b[i]),
            "fourier_w": fourier_w[i], "natten_w": natten_w[i], "w_out": w_out[i],
            "ln1_g": row(ln1_g[i]), "ln1_b": row(ln1_b[i]),
            "ffn_up": ffn_up[i], "ffn_conv": ffn_conv[i], "ffn_conv_b": row(ffn_conv_b[i]),
            "ffn_down": ffn_down[i], "ple_proj": ple_proj[i], "ple_gate": ple_gate[i],
            "ln2_g": row(ln2_g[i]), "ln2_b": row(ln2_b[i]),
        }
        bias, wts["w_in"] = _natten_bias(natten_rpb[i], w_in[i])
        xp, wts = _encoder_layer(xp, p_prompt[i, 0], wts, bias, _dft_tables(xp.shape[0]))
        xs, wts = _encoder_layer(xs, p_sample[i, 0], wts, bias, _dft_tables(xs.shape[0]))
    return (xp[None], xs[None])
```

```python
import functools
import math

import jax
import jax.numpy as jnp
import numpy as np
from jax import lax
from jax.experimental import pallas as pl
from jax.experimental.pallas import tpu as pltpu

D_MODEL = 2048
GRID_W = 64
PLE_DIM = 256
F_WIDTH = 1024
F_GROUPS = 8
F_GROUP_CH = F_WIDTH // F_GROUPS
NA_HEADS = 8
NA_HEAD_DIM = 128
NA_WIDTH = NA_HEADS * NA_HEAD_DIM
WIN_H = 8
WIN_W = 16
D_FF = 5632
IN_COLS = F_WIDTH + 3 * NA_WIDTH + 2 * D_MODEL
LN_EPS = 1e-5
DEPTH = 1
DN_ALPHA = (2.0 * DEPTH) ** 0.25

VMEM_PHYSICAL_BYTES = 64 * 1024 * 1024
VMEM_LIMIT_BYTES = 56 * 1024 * 1024
VMEM_INTERNAL_SCRATCH_BYTES = 6 * 1024 * 1024
NEG_BIAS = -1e30

DFT_N2 = 128
NA_QROWS = 8
NA_KROWS = 16
NA_KROW_OFFSET = 4


def _cparams(*semantics, vmem_limit_bytes=VMEM_LIMIT_BYTES):
    return pltpu.CompilerParams(dimension_semantics=semantics, vmem_limit_bytes=vmem_limit_bytes)


MXU_COLS = 256


CAST_ROW_ALIGN = 16


def _cast_rows_per_step(rows, n_steps):
    per_step = -(-rows // n_steps)
    per_step = -(-per_step // CAST_ROW_ALIGN) * CAST_ROW_ALIGN
    while rows % per_step:
        per_step += CAST_ROW_ALIGN
    return per_step


def _in_proj_kernel(*refs, tn, n_cast):
    x_ref, wz_ref, wg_ref, gb_ref = refs[:4]
    cast_in = refs[4:4 + n_cast]
    z_ref, g_ref = refs[4 + n_cast:6 + n_cast]
    cast_out = refs[6 + n_cast:6 + 2 * n_cast]
    xb_ref = refs[-1]

    @pl.when(pl.program_id(1) == 0)
    def _():
        xb_ref[...] = x_ref[...].astype(jnp.bfloat16)

    for src_ref, dst_ref in zip(cast_in, cast_out):
        dst_ref[...] = src_ref[...].astype(dst_ref.dtype)

    xb = xb_ref[...]
    for c in range(0, tn, MXU_COLS):
        sl = slice(c, c + MXU_COLS)
        acc = jnp.dot(xb, wg_ref[:, sl], preferred_element_type=jnp.float32)
        g_ref[:, sl] = jax.nn.sigmoid(acc + gb_ref[:, sl]).astype(g_ref.dtype)
    for c in range(0, tn, MXU_COLS):
        sl = slice(c, c + MXU_COLS)
        z_ref[:, sl] = jnp.dot(xb, wz_ref[:, sl], preferred_element_type=jnp.float32).astype(z_ref.dtype)


def _in_proj(x, w_in_b, gate_b, cast_weights=(), tm=1024, tn=1024):
    s = x.shape[0]
    z_cols = F_WIDTH + 3 * NA_WIDTH
    g_cols = IN_COLS - z_cols
    assert z_cols == g_cols
    n_col = z_cols // tn
    n_steps = (s // tm) * n_col
    cast_rows = [_cast_rows_per_step(w.shape[0], n_steps) for w in cast_weights]
    cast_blocks = tuple(w.shape[0] // r for w, r in zip(cast_weights, cast_rows))
    vmem_bytes = (2 * tm * D_MODEL * 4 + 2 * 2 * D_MODEL * tn * 2 + 2 * 2 * tm * tn * 2 + tm * D_MODEL * 2
                  + sum(2 * r * w.shape[1] * (4 + 2) for w, r in zip(cast_weights, cast_rows))
                  + VMEM_INTERNAL_SCRATCH_BYTES)
    assert vmem_bytes <= VMEM_PHYSICAL_BYTES - VMEM_INTERNAL_SCRATCH_BYTES // 2, vmem_bytes
    cast_specs = [
        pl.BlockSpec((r, w.shape[1]), lambda i, j, nb=nb: (jnp.minimum(i * n_col + j, nb - 1), 0))
        for w, r, nb in zip(cast_weights, cast_rows, cast_blocks)]
    outs = pl.pallas_call(
        functools.partial(_in_proj_kernel, tn=tn, n_cast=len(cast_weights)),
        grid=(s // tm, n_col),
        in_specs=[
            pl.BlockSpec((tm, D_MODEL), lambda i, j: (i, 0)),
            pl.BlockSpec((D_MODEL, tn), lambda i, j: (0, j)),
            pl.BlockSpec((D_MODEL, tn), lambda i, j: (0, j + n_col)),
            pl.BlockSpec((1, tn), lambda i, j: (0, j)),
        ] + cast_specs,
        out_specs=[pl.BlockSpec((tm, tn), lambda i, j: (i, j)),
                   pl.BlockSpec((tm, tn), lambda i, j: (i, j))] + cast_specs,
        out_shape=[jax.ShapeDtypeStruct((s, z_cols), jnp.bfloat16),
                   jax.ShapeDtypeStruct((s, g_cols), jnp.bfloat16)]
        + [jax.ShapeDtypeStruct(w.shape, jnp.bfloat16) for w in cast_weights],
        scratch_shapes=[pltpu.VMEM((tm, D_MODEL), jnp.bfloat16)],
        compiler_params=_cparams("arbitrary", "arbitrary", vmem_limit_bytes=vmem_bytes),
        name="in_proj",
    )(x, w_in_b, w_in_b, gate_b, *cast_weights)
    return outs[0], outs[1], tuple(outs[2:])


def _dft_tables(s):
    n2 = DFT_N2
    n1 = s // n2
    k1 = np.arange(n1)
    w1 = np.exp(-2j * np.pi * np.outer(k1, k1) / n1) / math.sqrt(n1)
    fa = np.concatenate([w1.real, w1.imag], axis=0)
    k2 = np.arange(n2)
    w2 = np.exp(-2j * np.pi * np.outer(k2, k2) / n2) / math.sqrt(n2)
    tw = np.exp(-2j * np.pi * np.outer(k1, k2) / s)
    c = np.arange(F_GROUP_CH)
    wc = np.exp(-2j * np.pi * np.outer(c, c) / F_GROUP_CH) / math.sqrt(F_GROUP_CH)
    wch = np.concatenate([wc.real, -wc.imag], axis=0)
    f32 = lambda a: jnp.asarray(a, dtype=jnp.float32)
    return (f32(fa), f32(w2.real), f32(w2.imag), f32(tw.real), f32(tw.imag), f32(wch))


DFT_SLABS = 16


def _dft_pos_a_kernel(fa_ref, x_ref, o_ref, *, n1):
    fa = fa_ref[...].astype(jnp.bfloat16)
    xt = jnp.swapaxes(x_ref[...], 0, 1)
    re, im = [], []
    for j in range(DFT_SLABS):
        p = jnp.dot(fa, xt[j], preferred_element_type=jnp.float32)
        re.append(p[:n1].astype(o_ref.dtype))
        im.append(p[n1:].astype(o_ref.dtype))
    o_ref[0] = jnp.swapaxes(jnp.stack(re, axis=0), 0, 1)
    o_ref[1] = jnp.swapaxes(jnp.stack(im, axis=0), 0, 1)


def _dft_pos_a(z3, fa):
    n1, n2, _ = z3.shape
    return pl.pallas_call(
        functools.partial(_dft_pos_a_kernel, n1=n1),
        grid=(n2 // DFT_SLABS,),
        in_specs=[
            pl.BlockSpec((2 * n1, n1), lambda i: (0, 0)),
            pl.BlockSpec((n1, DFT_SLABS, F_WIDTH), lambda i: (0, i, 0)),
        ],
        out_specs=pl.BlockSpec((2, n1, DFT_SLABS, F_WIDTH), lambda i: (0, 0, i, 0)),
        out_shape=jax.ShapeDtypeStruct((2, n1, n2, F_WIDTH), jnp.bfloat16),
        compiler_params=_cparams("parallel"),
        name="dft_pos_a",
    )(fa, z3)


def _dft_pos_c_kernel(w2r_ref, w2i_ref, twr_ref, twi_ref, wch_ref, a_ref, o_ref):
    w2r = w2r_ref[...]
    w2i = w2i_ref[...]
    wch = wch_ref[...].astype(jnp.bfloat16)
    def position_dft(j):
        tr = twr_ref[j:j + 1, :]
        ti = twi_ref[j:j + 1, :]
        gr = (w2r * tr - w2i * ti).astype(jnp.bfloat16)
        gi = (w2r * ti + w2i * tr).astype(jnp.bfloat16)
        g4 = jnp.concatenate(
            [jnp.concatenate([gr, -gi], axis=1), jnp.concatenate([gi, gr], axis=1)], axis=0)
        a = jnp.concatenate([a_ref[0, j], a_ref[1, j]], axis=0)
        return jnp.dot(g4, a, preferred_element_type=jnp.float32).astype(jnp.bfloat16)

    def channel_dft(y):
        n2 = y.shape[0] // 2
        yr, yi = y[:n2], y[n2:]
        groups = []
        for g in range(F_GROUPS):
            sl = slice(g * F_GROUP_CH, (g + 1) * F_GROUP_CH)
            lhs = jnp.concatenate([yr[:, sl], yi[:, sl]], axis=1)
            groups.append(jnp.dot(lhs, wch, preferred_element_type=jnp.float32).astype(o_ref.dtype))
        return jnp.concatenate(groups, axis=1)

    slabs = []
    y_next = position_dft(0)
    for j in range(DFT_SLABS):
        y = y_next
        if j + 1 < DFT_SLABS:
            y_next = position_dft(j + 1)
        slabs.append(channel_dft(y))
    o_ref[...] = jnp.swapaxes(jnp.stack(slabs, axis=0), 0, 1)


def _dft_pos_c(a, w2r, w2i, twr, twi, wch):
    _, n1, n2, c = a.shape
    full = lambda shape: pl.BlockSpec(shape, lambda i: (0,) * len(shape))
    return pl.pallas_call(
        _dft_pos_c_kernel,
        grid=(n1 // DFT_SLABS,),
        in_specs=[
            full((n2, n2)), full((n2, n2)),
            pl.BlockSpec((DFT_SLABS, n2), lambda i: (i, 0)),
            pl.BlockSpec((DFT_SLABS, n2), lambda i: (i, 0)),
            full((2 * F_GROUP_CH, F_GROUP_CH)),
            pl.BlockSpec((2, DFT_SLABS, n2, c), lambda i: (0, i, 0, 0)),
        ],
        out_specs=pl.BlockSpec((n2, DFT_SLABS, c), lambda i: (0, i, 0)),
        out_shape=jax.ShapeDtypeStruct((n2, n1, c), jnp.bfloat16),
        compiler_params=_cparams("parallel"),
        name="dft_pos_c",
    )(w2r, w2i, twr, twi, wch, a)


def _fourier_mix(z, tables):
    s = z.shape[0]
    n2 = DFT_N2
    n1 = s // n2
    fa, w2r, w2i, twr, twi, wch = tables
    a = _dft_pos_a(z.reshape(n1, n2, z.shape[1]), fa)
    y = _dft_pos_c(a, w2r, w2i, twr, twi, wch)
    return y.reshape(s, F_WIDTH)


NA_RPB_H = 2 * WIN_H - 1
NA_RPB_W = 2 * WIN_W - 1
NA_KROW_PAIRS = NA_KROWS // 2


def _natten_row_windows():
    i = np.arange(NA_QROWS)
    mid = np.full(NA_QROWS, -(WIN_H // 2))
    first = np.where(i < WIN_H // 2, -i, mid)
    last = np.where(i > WIN_H // 2, -i, mid)
    return (first, mid, last)


def _natten_bias_kernel(rpb_ref, w_ref, o_ref, wb_ref, cb_ref):
    h = pl.program_id(0)
    wb_ref[...] = w_ref[...].astype(wb_ref.dtype)
    shape = (GRID_W, 2 * GRID_W)
    qc = lax.broadcasted_iota(jnp.int32, shape, 0)
    lane = lax.broadcasted_iota(jnp.int32, shape, 1)
    kc = lane & (GRID_W - 1)
    col_start = jnp.clip(qc - WIN_W // 2, 0, GRID_W - WIN_W)
    col_ok = (kc >= col_start) & (kc < col_start + WIN_W)
    dc = jnp.where(col_ok, kc - qc + WIN_W - 1, -1)
    neg = jnp.full(shape, NEG_BIAS, jnp.float32)
    for d in range(NA_RPB_H):
        tile = neg
        for t in range(NA_RPB_W):
            r = rpb_ref[(h * NA_RPB_H + d) * NA_RPB_W + t] * (NA_HEAD_DIM ** 0.5)
            tile = jnp.where(dc == t, r, tile)
        cb_ref[d] = tile
    left = lane < GRID_W
    for v, lo in enumerate(_natten_row_windows()):
        for i in range(NA_QROWS):
            for jj in range(NA_KROW_PAIRS):
                dr = (2 * jj - NA_KROW_OFFSET - i, 2 * jj + 1 - NA_KROW_OFFSET - i)
                ok = [lo[i] <= d <= lo[i] + WIN_H - 1 for d in dr]
                halves = [cb_ref[d + WIN_H - 1] if good else neg for d, good in zip(dr, ok)]
                tile = jnp.where(left, halves[0], halves[1]) if any(ok) else neg
                o_ref[v, 0, i * GRID_W:(i + 1) * GRID_W, jj * 2 * GRID_W:(jj + 1) * 2 * GRID_W] = tile


def _natten_bias(rpb, w_in):
    blk = NA_QROWS * GRID_W
    keys = NA_KROWS * GRID_W
    rows = _cast_rows_per_step(w_in.shape[0], NA_HEADS)
    assert w_in.shape[0] == rows * NA_HEADS
    wspec = pl.BlockSpec((rows, w_in.shape[1]), lambda h: (h, 0))
    return pl.pallas_call(
        _natten_bias_kernel,
        grid=(NA_HEADS,),
        in_specs=[pl.BlockSpec(memory_space=pltpu.SMEM), wspec],
        out_specs=[pl.BlockSpec((3, 1, blk, keys), lambda h: (0, h, 0, 0)), wspec],
        out_shape=[jax.ShapeDtypeStruct((3, NA_HEADS, blk, keys), jnp.float32),
                   jax.ShapeDtypeStruct(w_in.shape, jnp.bfloat16)],
        scratch_shapes=[pltpu.VMEM((NA_RPB_H, GRID_W, 2 * GRID_W), jnp.float32)],
        compiler_params=_cparams("parallel"),
        name="natten_bias",
    )(rpb.astype(jnp.float32).reshape(-1), w_in)


NA_HEADS_PER_STEP = 4


NA_ROW_BLOCK = 32


def _natten_kernel(q_ref, kp_ref, kc_ref, kn_ref, vp_ref, vc_ref, vn_ref, b_ref, o_ref, t_ref, p_ref):
    half = (NA_KROWS - NA_QROWS) // 2 * GRID_W
    blk = NA_QROWS * GRID_W
    nq = blk // 2
    nkeys = blk + half
    exp2_scale = (NA_HEAD_DIM ** -0.5) * math.log2(math.e)
    ones = jnp.ones((nkeys, NA_HEAD_DIM), jnp.bfloat16)
    chains = []
    for h in range(NA_HEADS_PER_STEP):
        sl = slice(h * NA_HEAD_DIM, (h + 1) * NA_HEAD_DIM)
        chains.append((h, sl, 0, 0, (kp_ref, kc_ref), (vp_ref, vc_ref)))
        chains.append((h, sl, nq, half, (kc_ref, kn_ref), (vc_ref, vn_ref)))

    def scores(ch):
        h, sl, row0, key0, k_refs, v_refs = chains[ch]
        kwin = jnp.concatenate([r[:, sl] for r in k_refs], axis=0)
        t_ref[ch] = lax.dot_general(q_ref[row0:row0 + nq, sl], kwin, (((1,), (1,)), ((), ())),
                                    preferred_element_type=jnp.float32)

    def softmax(ch):
        h, sl, row0, key0, k_refs, v_refs = chains[ch]
        biased = lambda r0: (t_ref[ch, r0:r0 + NA_ROW_BLOCK, :]
                             + b_ref[0, h, row0 + r0:row0 + r0 + NA_ROW_BLOCK, key0:key0 + nkeys])
        row_blocks = range(0, nq, NA_ROW_BLOCK)
        maxes = [jnp.max(biased(r0), axis=-1, keepdims=True) for r0 in row_blocks]
        for m, r0 in zip(maxes, row_blocks):
            p_ref[ch, r0:r0 + NA_ROW_BLOCK, :] = jnp.exp2((biased(r0) - m) * exp2_scale).astype(p_ref.dtype)

    def weighted_values(ch):
        h, sl, row0, key0, k_refs, v_refs = chains[ch]
        vwin = jnp.concatenate([r[:, sl] for r in v_refs], axis=0)
        ol = jnp.dot(p_ref[ch], jnp.concatenate([vwin, ones], axis=1), preferred_element_type=jnp.float32)
        o_ref[row0:row0 + nq, sl] = (ol[:, :NA_HEAD_DIM] / ol[:, NA_HEAD_DIM:]).astype(o_ref.dtype)

    scores(0)
    for ch in range(len(chains)):
        if ch + 1 < len(chains):
            scores(ch + 1)
        softmax(ch)
        weighted_values(ch)


def _natten(z, bias):
    s = z.shape[0]
    blk = NA_QROWS * GRID_W
    nb = s // blk
    width = NA_HEADS_PER_STEP * NA_HEAD_DIM
    q0 = F_WIDTH // width
    k0 = q0 + NA_WIDTH // width
    v0 = k0 + NA_WIDTH // width
    half = (NA_KROWS - NA_QROWS) // 2 * GRID_W
    assert blk == 2 * half
    prev = lambda b: jnp.maximum(2 * b - 1, 0)
    nxt = lambda b: jnp.minimum(2 * b + 2, 2 * nb - 1)
    variant = lambda b: jnp.where(b == 0, 0, jnp.where(b == nb - 1, 2, 1))
    spec = lambda col0, rowfn, rows=blk: pl.BlockSpec((rows, width), lambda h, b: (rowfn(b), col0 + h))
    same = lambda b: b
    return pl.pallas_call(
        _natten_kernel,
        grid=(NA_HEADS // NA_HEADS_PER_STEP, nb),
        in_specs=[
            spec(q0, same),
            spec(k0, prev, half), spec(k0, same), spec(k0, nxt, half),
            spec(v0, prev, half), spec(v0, same), spec(v0, nxt, half),
            pl.BlockSpec((1, NA_HEADS_PER_STEP, blk, NA_KROWS * GRID_W), lambda h, b: (variant(b), h, 0, 0)),
        ],
        out_specs=pl.BlockSpec((blk, width), lambda h, b: (b, h)),
        out_shape=jax.ShapeDtypeStruct((s, NA_WIDTH), jnp.bfloat16),
        scratch_shapes=[pltpu.VMEM((2 * NA_HEADS_PER_STEP, blk // 2, blk + blk // 2), jnp.float32),
                        pltpu.VMEM((2 * NA_HEADS_PER_STEP, blk // 2, blk + blk // 2), jnp.bfloat16)],
        compiler_params=_cparams("parallel", "arbitrary"),
        name="natten",
    )(z, z, z, z, z, z, z, bias)


def _layer_norm(h, g, b):
    mu = jnp.mean(h, axis=-1, keepdims=True)
    d = h - mu
    var = jnp.mean(d * d, axis=-1, keepdims=True)
    return d * lax.rsqrt(var + LN_EPS) * g + b


def _mix_ln1_kernel(x_ref, fm_ref, at_ref, ga_ref, gb_ref, p_ref, fw_ref, nw_ref, wo_ref, pg_ref, pp_ref,
                    g_ref, b_ref, r_ref, x1b_ref, h_ref):
    @pl.when(pl.program_id(0) == 0)
    def _():
        h_ref[...] = jnp.zeros_like(h_ref)

    a = jnp.dot(fm_ref[...], fw_ref[...], preferred_element_type=jnp.float32)
    b = jnp.dot(at_ref[...], nw_ref[...], preferred_element_type=jnp.float32)
    emb = jnp.dot(p_ref[...].astype(jnp.bfloat16), pp_ref[...], preferred_element_type=jnp.float32)

    x1 = _layer_norm(h_ref[...], g_ref[...], b_ref[...])
    x1b = x1.astype(jnp.bfloat16)
    gate = jax.nn.sigmoid(jnp.dot(x1b, pg_ref[...], preferred_element_type=jnp.float32))
    r_ref[...] = DN_ALPHA * x1 + gate * emb
    x1b_ref[...] = x1b

    merged = ga_ref[...].astype(jnp.float32) * a + gb_ref[...].astype(jnp.float32) * b
    h_ref[...] = DN_ALPHA * x_ref[...] + jnp.dot(merged.astype(jnp.bfloat16), wo_ref[...],
                                                 preferred_element_type=jnp.float32)


def _resident(shape):
    return pl.BlockSpec(shape, lambda *_: (0,) * len(shape), pipeline_mode=pl.Buffered(1))


def _mix_ln1(x, fmix, att, gates, p, fw_b, nw_b, wo_b, pg_b, pp_b, ln_g, ln_b, tm=256):
    s = x.shape[0]
    n = s // tm
    cur = lambda width, col=0: pl.BlockSpec((tm, width), lambda i: (jnp.minimum(i, n - 1), col))
    lag = lambda width: pl.BlockSpec((tm, width), lambda i: (jnp.maximum(i - 1, 0), 0))
    return pl.pallas_call(
        _mix_ln1_kernel,
        grid=(n + 1,),
        in_specs=[
            cur(D_MODEL), cur(F_WIDTH), cur(NA_WIDTH), cur(D_MODEL, 0), cur(D_MODEL, 1), lag(PLE_DIM),
            _resident((F_WIDTH, D_MODEL)), _resident((NA_WIDTH, D_MODEL)), _resident((D_MODEL, D_MODEL)),
            _resident((D_MODEL, D_MODEL)), _resident((PLE_DIM, D_MODEL)),
            _resident((1, D_MODEL)), _resident((1, D_MODEL)),
        ],
        out_specs=[lag(D_MODEL), lag(D_MODEL)],
        out_shape=[jax.ShapeDtypeStruct((s, D_MODEL), jnp.float32),
                   jax.ShapeDtypeStruct((s, D_MODEL), jnp.bfloat16)],
        scratch_shapes=[pltpu.VMEM((tm, D_MODEL), jnp.float32)],
        compiler_params=_cparams("arbitrary"),
        name="mix_ln1",
    )(x, fmix, att, gates, gates, p, fw_b, nw_b, wo_b, pg_b, pp_b, ln_g, ln_b)


FFN_HALO = 16


def _gelu_tanh(u):
    k = math.sqrt(2.0 / math.pi)
    return 0.5 * u * (1.0 + jnp.tanh(u * (k + (k * 0.044715) * (u * u))))


LANES = 128
FFN_ROW_BLOCK = 128
FFN_STAGE_SLOTS = 4


def _ffn_up_kernel(xp_ref, xc_ref, xn_ref, wa_ref, wb_ref, ca_ref, cb_ref, ba_ref, bb_ref, h_ref,
                   xe_ref, ua_ref, ub_ref, *, tm, tn, n_row_tiles):
    i = pl.program_id(1)
    xe_ref[:FFN_HALO, :] = jnp.where(i > 0, xp_ref[...], jnp.zeros_like(xp_ref))
    xe_ref[FFN_HALO:FFN_HALO + tm, :] = xc_ref[...]
    xe_ref[FFN_HALO + tm:, :] = jnp.where(i < n_row_tiles - 1, xn_ref[...], jnp.zeros_like(xn_ref))
    lanes_per_chunk = MXU_COLS // LANES

    def matmul_chunk(c, slot):
        sl = slice(c * MXU_COLS, (c + 1) * MXU_COLS)
        for w_ref, u_ref in ((wa_ref, ua_ref), (wb_ref, ub_ref)):
            u = jnp.dot(xe_ref[...], w_ref[:, sl], preferred_element_type=jnp.float32)
            for k in range(lanes_per_chunk):
                u_ref[slot, k] = u[:, k * LANES:(k + 1) * LANES]

    def epilogue_chunk(c, slot):
        for k in range(lanes_per_chunk):
            sl = slice(c * MXU_COLS + k * LANES, c * MXU_COLS + (k + 1) * LANES)
            taps = []
            for c_ref, b_ref in ((ca_ref, ba_ref), (cb_ref, bb_ref)):
                taps.append([jnp.broadcast_to(c_ref[t:t + 1, sl], (FFN_ROW_BLOCK, LANES)) for t in range(3)]
                            + [jnp.broadcast_to(b_ref[:, sl], (FFN_ROW_BLOCK, LANES))])
            for r in range(0, tm, FFN_ROW_BLOCK):
                halves = []
                for u_ref, (c0, c1, c2, bias) in zip((ua_ref, ub_ref), taps):
                    lo = u_ref[slot, k, pl.ds(FFN_HALO - 1 + r, FFN_ROW_BLOCK), :]
                    mid = u_ref[slot, k, pl.ds(FFN_HALO + r, FFN_ROW_BLOCK), :]
                    hi = u_ref[slot, k, pl.ds(FFN_HALO + 1 + r, FFN_ROW_BLOCK), :]
                    halves.append(lo * c0 + mid * c1 + hi * c2 + bias)
                h_ref[pl.ds(r, FFN_ROW_BLOCK), sl] = (_gelu_tanh(halves[0]) * halves[1]).astype(h_ref.dtype)

    n_chunks = tn // MXU_COLS
    for c in range(n_chunks):
        matmul_chunk(c, c % FFN_STAGE_SLOTS)
        if c > 0:
            epilogue_chunk(c - 1, (c - 1) % FFN_STAGE_SLOTS)
    epilogue_chunk(n_chunks - 1, (n_chunks - 1) % FFN_STAGE_SLOTS)


def _ffn_up(x1b, up_b, conv, conv_b, tm=1024, tn=D_FF // 2):
    s = x1b.shape[0]
    n_row_tiles = s // tm
    n_col = D_FF // tn
    halo_per_tile = tm // FFN_HALO
    n_halo_blocks = s // FFN_HALO
    prev = lambda j, i: (jnp.maximum(i * halo_per_tile - 1, 0), 0)
    nxt = lambda j, i: (jnp.minimum((i + 1) * halo_per_tile, n_halo_blocks - 1), 0)
    wspec = lambda shape, off: pl.BlockSpec(shape, lambda j, i: (0, j + off), pipeline_mode=pl.Buffered(1))
    return pl.pallas_call(
        functools.partial(_ffn_up_kernel, tm=tm, tn=tn, n_row_tiles=n_row_tiles),
        grid=(n_col, n_row_tiles),
        in_specs=[
            pl.BlockSpec((FFN_HALO, D_MODEL), prev),
            pl.BlockSpec((tm, D_MODEL), lambda j, i: (i, 0)),
            pl.BlockSpec((FFN_HALO, D_MODEL), nxt),
            wspec((D_MODEL, tn), 0), wspec((D_MODEL, tn), n_col),
            wspec((3, tn), 0), wspec((3, tn), n_col),
            wspec((1, tn), 0), wspec((1, tn), n_col),
        ],
        out_specs=pl.BlockSpec((tm, tn), lambda j, i: (i, j)),
        out_shape=jax.ShapeDtypeStruct((s, D_FF), jnp.bfloat16),
        scratch_shapes=[pltpu.VMEM((tm + 2 * FFN_HALO, D_MODEL), jnp.bfloat16),
                        pltpu.VMEM((FFN_STAGE_SLOTS, MXU_COLS // LANES, tm + 2 * FFN_HALO, LANES), jnp.float32),
                        pltpu.VMEM((FFN_STAGE_SLOTS, MXU_COLS // LANES, tm + 2 * FFN_HALO, LANES), jnp.float32)],
        compiler_params=_cparams("arbitrary", "arbitrary"),
        name="ffn_up",
    )(x1b, x1b, x1b, up_b, up_b, conv, conv, conv_b, conv_b)


SUBLANES = 8


def _zero_after(x):
    rows, cols = x.shape
    bits = pltpu.bitcast(x, jnp.uint32)
    acc = bits[:SUBLANES]
    for r in range(SUBLANES, rows, SUBLANES):
        acc = acc | bits[r:r + SUBLANES]
    tile = acc[:, :LANES]
    for c in range(LANES, cols, LANES):
        tile = tile | acc[:, c:c + LANES]
    zeros = pltpu.bitcast((tile >> 16) >> 16, jnp.float32)
    column = jnp.concatenate([zeros] * (rows // SUBLANES), axis=0)
    return jnp.concatenate([column] * (cols // LANES), axis=1)


def _ffn_down_ln2_kernel(r_ref, h_ref, wd_ref, g_ref, b_ref, o_ref, sum_ref):
    @pl.when(pl.program_id(0) == 0)
    def _():
        sum_ref[...] = jnp.zeros_like(sum_ref)

    y = _layer_norm(sum_ref[...], g_ref[...], b_ref[...])
    o_ref[...] = y
    sum_ref[...] = (r_ref[...] + _zero_after(y)) + jnp.dot(h_ref[...], wd_ref[...],
                                                           preferred_element_type=jnp.float32)


def _ffn_down_ln2(r, h, wd_b, ln_g, ln_b, tm=256):
    s = r.shape[0]
    n = s // tm
    cur = lambda width: pl.BlockSpec((tm, width), lambda i: (jnp.minimum(i, n - 1), 0))
    return pl.pallas_call(
        _ffn_down_ln2_kernel,
        grid=(n + 1,),
        in_specs=[cur(D_MODEL), cur(D_FF), _resident((D_FF, D_MODEL)),
                  _resident((1, D_MODEL)), _resident((1, D_MODEL))],
        out_specs=pl.BlockSpec((tm, D_MODEL), lambda i: (jnp.maximum(i - 1, 0), 0)),
        out_shape=jax.ShapeDtypeStruct((s, D_MODEL), jnp.float32),
        scratch_shapes=[pltpu.VMEM((tm, D_MODEL), jnp.float32)],
        compiler_params=_cparams("arbitrary"),
        name="ffn_down_ln2",
    )(r, h, wd_b, ln_g, ln_b)


CAST_IN_KERNEL = ("fourier_w", "natten_w", "w_out", "ple_gate", "ple_proj", "ffn_up", "ffn_down")


def _encoder_layer(x, p, wts, bias, tables):
    pending = [name for name in CAST_IN_KERNEL if wts[name].dtype != jnp.bfloat16]
    z, gates, cast = _in_proj(x, wts["w_in"], wts["gate_b"], [wts[name] for name in pending])
    wts = {**wts, **dict(zip(pending, cast))}
    return _layer_after_in_proj(x, z, gates, p, wts, bias, tables), wts


def _layer_after_in_proj(x, z, gates, p, wts, bias, tables):
    fmix = _fourier_mix(z, tables)
    att = _natten(z, bias)
    r, x1b = _mix_ln1(x, fmix, att, gates, p, wts["fourier_w"], wts["natten_w"], wts["w_out"],
                      wts["ple_gate"], wts["ple_proj"], wts["ln1_g"], wts["ln1_b"])
    h = _ffn_up(x1b, wts["ffn_up"], wts["ffn_conv"], wts["ffn_conv_b"])
    return _ffn_down_ln2(r, h, wts["ffn_down"], wts["ln2_g"], wts["ln2_b"])


def kernel(x_prompt, x_sample, p_prompt, p_sample, w_in, gate_b, fourier_w, natten_rpb, natten_w, w_out,
           ln1_g, ln1_b, ffn_up, ffn_conv, ffn_conv_b, ffn_down, ple_proj, ple_gate, ln2_g, ln2_b):
    assert w_in.shape[0] == DEPTH
    xp, xs = x_prompt[0], x_sample[0]
    row = lambda v: v.reshape(1, -1)
    for i in range(DEPTH):
        wts = {
            "gate_b": row(gate_b[i]),
            "fourier_w": fourier_w[i], "natten_w": natten_w[i], "w_out": w_out[i],
            "ln1_g": row(ln1_g[i]), "ln1_b": row(ln1_b[i]),
            "ffn_up": ffn_up[i], "ffn_conv": ffn_conv[i], "ffn_conv_b": row(ffn_conv_b[i]),
            "ffn_down": ffn_down[i], "ple_proj": ple_proj[i], "ple_gate": ple_gate[i],
            "ln2_g": row(ln2_g[i]), "ln2_b": row(ln2_b[i]),
        }
        bias, wts["w_in"] = _natten_bias(natten_rpb[i], w_in[i])
        xp, wts = _encoder_layer(xp, p_prompt[i, 0], wts, bias, _dft_tables(xp.shape[0]))
        xs, wts = _encoder_layer(xs, p_sample[i, 0], wts, bias, _dft_tables(xs.shape[0]))
    return (xp[None], xs[None])
```
